```python
import jax
import jax.numpy as jnp
from jax import lax
import numpy as np

D_MODEL = 1024
BATCH = 4
SEQ = 4096
DEPTH = 2
DEC_BATCH = 128
DEC_SEQ = 1
PAST_LEN = 2048
PAGE_SIZE = 128

N_MIXERS = 2
N_HEADS = 16
HEAD_DIM = 64
N_KV = 4
GROUP = N_HEADS // N_KV
Q_DIM = N_HEADS * HEAD_DIM
KV_DIM = N_KV * HEAD_DIM
ROT_DIM = HEAD_DIM // 4
ROPE_THETA = 500000.0
D_FF = 2816
BLOCK = 64
N_SEL = 8
WINDOW = 512
CMP_HIDDEN = 128
FORCE = 1.0e4
N_IDX = 8
IDX_DIM = 64
DSA_TOPK = 256
Q_BLOCK = 128
NSA_COLS = Q_DIM + 6 * KV_DIM + 3 * N_HEADS
DSA_COLS = Q_DIM + 2 * KV_DIM + N_IDX * IDX_DIM + IDX_DIM + N_IDX
RMS_EPS = 1e-6
NEG = -1e30

kernel_name = 'nsa_dsa_macaron_hybrid_step'


def rmsnorm(x, g):
    xf = x.astype(jnp.float32)
    y = xf * lax.rsqrt(jnp.mean(xf * xf, axis=-1, keepdims=True) + RMS_EPS)
    return y.astype(x.dtype) * g


def swiglu(h, w_in, w_out):
    gu = h @ w_in
    return (jax.nn.silu(gu[..., :D_FF]) * gu[..., D_FF:]) @ w_out


def rope(x, pos):
    half = ROT_DIM // 2
    inv = ROPE_THETA ** (-jnp.arange(half, dtype=jnp.float32) * 2.0 / ROT_DIM)
    ang = pos.astype(jnp.float32)[:, None] * inv[None, :]
    cos = jnp.cos(ang)[:, None, :].astype(x.dtype)
    sin = jnp.sin(ang)[:, None, :].astype(x.dtype)
    x1, x2, xp = x[..., :half], x[..., half:ROT_DIM], x[..., ROT_DIM:]
    return jnp.concatenate([x1 * cos - x2 * sin, x2 * cos + x1 * sin, xp], axis=-1)


def masked_softmax(s, mask):
    s = jnp.where(mask, s, NEG)
    p = jnp.where(mask, jnp.exp(s - jnp.max(s, axis=-1, keepdims=True)), 0.0)
    return p / jnp.maximum(jnp.sum(p, axis=-1, keepdims=True), 1e-30)


def gather_paged(pool, page_table, new_rows, pos, head=None):
    n_past = page_table.shape[1] * PAGE_SIZE
    bidx = jnp.arange(pos.shape[0]).reshape((-1,) + (1,) * (pos.ndim - 1))
    pp = jnp.clip(pos, 0, n_past - 1)
    phys = page_table[bidx, pp // PAGE_SIZE]
    off = pp % PAGE_SIZE
    nidx = jnp.clip(pos - n_past, 0, new_rows.shape[1] - 1)
    if head is None:
        past = pool[phys, off]
        new = new_rows[bidx, nidx]
    else:
        past = pool[phys, off, :, head]
        new = new_rows[bidx, nidx, :, head]
    in_past = (pos < n_past).reshape(pos.shape + (1,) * (past.ndim - pos.ndim))
    return jnp.where(in_past, past, new)


def compress(rows, pos_emb, w1, w2):
    b, lp = rows.shape[:2]
    blocks = rows.reshape(b, lp // BLOCK, BLOCK, 2, N_KV, HEAD_DIM)
    pre = (jnp.einsum('bnlckd,cldh->bnckh', blocks, w1)
           + jnp.einsum('lcd,cldh->ch', pos_emb, w1)[:, None, :])
    return jnp.einsum('bnckh,chd->bnckd', jax.nn.gelu(pre), w2)


def nsa_attend(q, qr, gates, qpos, kc, vc, sel_fn, kw, vw, kwpos):
    scale = HEAD_DIM ** -0.5
    nb = kc.shape[1]
    blk = jnp.arange(nb)
    s = jnp.einsum('btkgd,bnkd->btkgn', q, kc).astype(jnp.float32) * scale
    cmask = ((blk + 1) * BLOCK - 1)[None, :] <= qpos[:, None]
    p_cmp = masked_softmax(s, cmask[None, :, None, None, :])
    o_cmp = jnp.einsum('btkgn,bnkd->btkgd', p_cmp.astype(vc.dtype), vc)
    cur = (qpos // BLOCK)[:, None]
    imp = jnp.sum(p_cmp, axis=3)
    forced = (blk[None] == 0) | (blk[None] == cur) | (blk[None] == cur - 1)
    score = jnp.where(forced[None, :, None, :], FORCE, imp)
    score = jnp.where((blk[None] <= cur)[None, :, None, :], score, -1.0)
    nsel = min(N_SEL, nb)
    top_s, top_i = lax.top_k(score, nsel)
    spos5 = top_i[..., None] * BLOCK + jnp.arange(BLOCK)
    smask5 = (top_s >= 0)[..., None] & (spos5 <= qpos[None, :, None, None, None])
    flat = top_i.shape[:3] + (nsel * BLOCK,)
    spos, smask = spos5.reshape(flat), smask5.reshape(flat)
    ks, vs = sel_fn(spos)
    s = jnp.einsum('btkgd,btkmd->btkgm', qr, ks).astype(jnp.float32) * scale
    p = masked_softmax(s, smask[:, :, :, None, :])
    o_slc = jnp.einsum('btkgm,btkmd->btkgd', p.astype(vs.dtype), vs)
    s = jnp.einsum('btkgd,blkd->btkgl', qr, kw).astype(jnp.float32) * scale
    dpos = qpos[:, None] - kwpos[None, :]
    wmask = (dpos >= 0) & (dpos <= WINDOW) & (kwpos[None, :] >= 0)
    p = masked_softmax(s, wmask[None, :, None, None, :])
    o_win = jnp.einsum('btkgl,blkd->btkgd', p.astype(vw.dtype), vw)
    g = gates.astype(o_cmp.dtype)
    return g[..., 0:1] * o_cmp + g[..., 1:2] * o_slc + g[..., 2:3] * o_win


def nsa_project(h, w_in, pos):
    b, t, _ = h.shape
    z = h @ w_in
    q = z[..., :Q_DIM].reshape(b, t, N_HEADS, HEAD_DIM)
    kv = z[..., Q_DIM:Q_DIM + 6 * KV_DIM].reshape(b, t, 3, 2, N_KV, HEAD_DIM)
    gates = jax.nn.sigmoid(z[..., Q_DIM + 6 * KV_DIM:]).reshape(b, t, N_KV, GROUP, 3)
    qr = rope(q, pos).reshape(b, t, N_KV, GROUP, HEAD_DIM)
    q = q.reshape(b, t, N_KV, GROUP, HEAD_DIM)
    kv_cmp = kv[:, :, 0]
    kv_slc = jnp.stack([rope(kv[:, :, 1, 0], pos), kv[:, :, 1, 1]], axis=2)
    kv_win = jnp.stack([rope(kv[:, :, 2, 0], pos), kv[:, :, 2, 1]], axis=2)
    return q, qr, gates, kv_cmp, kv_slc, kv_win


def nsa_prompt(h, w_in, w_out, cmp_pos, cmp_w1, cmp_w2):
    b, t, _ = h.shape
    q, qr, gates, kv_cmp, kv_slc, kv_win = nsa_project(h, w_in, jnp.arange(t))
    kvc = compress(kv_cmp, cmp_pos, cmp_w1, cmp_w2)
    kc, vc = kvc[:, :, 0], kvc[:, :, 1]
    win_pad = jnp.pad(kv_win, ((0, 0), (WINDOW, 0), (0, 0), (0, 0), (0, 0)))
    bidx = jnp.arange(b)[:, None, None, None]
    hidx = jnp.arange(N_KV)[None, None, :, None]

    def sel_fn(spos):
        r = kv_slc[bidx, spos, :, hidx]
        return r[..., 0, :], r[..., 1, :]

    def block(i):
        s0 = i * Q_BLOCK
        cut = lambda a: lax.dynamic_slice_in_dim(a, s0, Q_BLOCK, axis=1)
        kw = lax.dynamic_slice_in_dim(win_pad, s0, WINDOW + Q_BLOCK, axis=1)
        qpos = s0 + jnp.arange(Q_BLOCK)
        kwpos = s0 - WINDOW + jnp.arange(WINDOW + Q_BLOCK)
        return nsa_attend(cut(q), cut(qr), cut(gates), qpos, kc, vc, sel_fn,
                          kw[:, :, 0], kw[:, :, 1], kwpos)

    o = lax.map(block, jnp.arange(t // Q_BLOCK))
    o = jnp.moveaxis(o, 0, 1).reshape(b, t, Q_DIM)
    wb = min(WINDOW, t)
    return o @ w_out, kv_cmp, kv_slc, kv_win[:, t - wb:]


def nsa_sample(h, cache_cmp, cache_slc, win_state, page_table, w_in, w_out, cmp_pos, cmp_w1, cmp_w2):
    b, t, _ = h.shape
    n_past = page_table.shape[1] * PAGE_SIZE
    pos = n_past + jnp.arange(t)
    q, qr, gates, kv_cmp, kv_slc, kv_win = nsa_project(h, w_in, pos)
    total = n_past + t
    nb = -(-total // BLOCK)
    past = cache_cmp[page_table].reshape((b, n_past) + cache_cmp.shape[2:])
    rows = jnp.concatenate([past, kv_cmp], axis=1)
    rows = jnp.pad(rows, ((0, 0), (0, nb * BLOCK - total), (0, 0), (0, 0), (0, 0)))
    kvc = compress(rows, cmp_pos, cmp_w1, cmp_w2)
    kc, vc = kvc[:, :, 0], kvc[:, :, 1]
    hidx = jnp.arange(N_KV)[None, None, :, None]

    def sel_fn(spos):
        r = gather_paged(cache_slc, page_table, kv_slc, spos, hidx)
        return r[..., 0, :], r[..., 1, :]

    win = jnp.concatenate([win_state, kv_win], axis=1)
    wb = win_state.shape[1]
    kwpos = n_past - wb + jnp.arange(wb + t)
    o = nsa_attend(q, qr, gates, pos, kc, vc, sel_fn, win[:, :, 0], win[:, :, 1], kwpos)
    return o.reshape(b, t, Q_DIM) @ w_out, kv_cmp, kv_slc, win[:, t:]


def dsa_attend(q, qi, wi, qpos, ki, topk, gather_kv):
    n_keys = ki.shape[1]
    sc = jnp.einsum('bthd,bsd->bths', qi, ki).astype(jnp.float32) * IDX_DIM ** -0.5
    score = jnp.einsum('bths,bth->bts', jax.nn.relu(sc), wi.astype(jnp.float32))
    kpos = jnp.arange(n_keys)
    score = jnp.where(kpos[None, None, :] <= qpos[None, :, None], score, NEG)
    _, top_i = lax.top_k(score, topk)
    valid = top_i <= qpos[None, :, None]
    kv = gather_kv(top_i)
    k, v = kv[:, :, :, 0], kv[:, :, :, 1]
    s = jnp.einsum('btkgd,btmkd->btkgm', q, k).astype(jnp.float32) * HEAD_DIM ** -0.5
    p = masked_softmax(s, valid[:, :, None, None, :])
    return jnp.einsum('btkgm,btmkd->btkgd', p.astype(v.dtype), v)


def dsa_project(h, w_in, pos):
    b, t, _ = h.shape
    z = h @ w_in
    q = rope(z[..., :Q_DIM].reshape(b, t, N_HEADS, HEAD_DIM), pos).reshape(b, t, N_KV, GROUP, HEAD_DIM)
    kv = z[..., Q_DIM:Q_DIM + 2 * KV_DIM].reshape(b, t, 2, N_KV, HEAD_DIM)
    kv = jnp.stack([rope(kv[:, :, 0], pos), kv[:, :, 1]], axis=2)
    o1 = Q_DIM + 2 * KV_DIM
    qi = rope(z[..., o1:o1 + N_IDX * IDX_DIM].reshape(b, t, N_IDX, IDX_DIM), pos)
    o2 = o1 + N_IDX * IDX_DIM
    ki = rope(z[..., o2:o2 + IDX_DIM].reshape(b, t, 1, IDX_DIM), pos)[:, :, 0]
    wi = z[..., o2 + IDX_DIM:] * N_IDX ** -0.5
    return q, kv, qi, ki, wi


def dsa_prompt(h, w_in, w_out):
    b, t, _ = h.shape
    q, kv, qi, ki, wi = dsa_project(h, w_in, jnp.arange(t))
    topk = min(DSA_TOPK, t // 4)
    bidx = jnp.arange(b)[:, None, None]

    def gather_kv(idx):
        return kv[bidx, idx]

    def block(i):
        s0 = i * Q_BLOCK
        cut = lambda a: lax.dynamic_slice_in_dim(a, s0, Q_BLOCK, axis=1)
        qpos = s0 + jnp.arange(Q_BLOCK)
        return dsa_attend(cut(q), cut(qi), cut(wi), qpos, ki, topk, gather_kv)

    o = lax.map(block, jnp.arange(t // Q_BLOCK))
    o = jnp.moveaxis(o, 0, 1).reshape(b, t, Q_DIM)
    return o @ w_out, kv, ki


def dsa_sample(h, cache_kv, cache_idx, page_table, w_in, w_out):
    b, t, _ = h.shape
    n_past = page_table.shape[1] * PAGE_SIZE
    pos = n_past + jnp.arange(t)
    q, kv, qi, ki, wi = dsa_project(h, w_in, pos)
    ki_all = jnp.concatenate([cache_idx[page_table].reshape(b, n_past, IDX_DIM), ki], axis=1)
    topk = min(DSA_TOPK, (n_past + t) // 4)

    def gather_kv(idx):
        return gather_paged(cache_kv, page_table, kv, idx)

    o = dsa_attend(q, qi, wi, pos, ki_all, topk, gather_kv)
    return o.reshape(b, t, Q_DIM) @ w_out, kv, ki


def setup_inputs(seed: int = 0) -> dict:
    key = jax.random.key(seed)
    ks = jax.random.split(key, 20)
    n_a = len(range(0, DEPTH, N_MIXERS))
    n_b = len(range(1, DEPTH, N_MIXERS))
    n_pages = PAST_LEN // PAGE_SIZE
    n_used = DEC_BATCH * n_pages
    n_pool = n_used + n_used // 4
    win_buf = min(WINDOW, PAST_LEN)
    f32 = jnp.float32
    nrm = lambda k, shape, sc: jax.random.normal(k, shape, f32) * sc
    page_table = jax.random.permutation(ks[7], n_pool)[:n_used].reshape(DEC_BATCH, n_pages).astype(jnp.int32)
    return {
        'x_prompt': nrm(ks[0], (BATCH, SEQ, D_MODEL), 1.0),
        'x_sample': nrm(ks[1], (DEC_BATCH, DEC_SEQ, D_MODEL), 1.0),
        'cache_nsa_cmp': nrm(ks[2], (n_a, n_pool, PAGE_SIZE, 2, N_KV, HEAD_DIM), 1.0),
        'cache_nsa_slc': nrm(ks[3], (n_a, n_pool, PAGE_SIZE, 2, N_KV, HEAD_DIM), 1.0),
        'state_nsa_win': nrm(ks[4], (n_a, DEC_BATCH, win_buf, 2, N_KV, HEAD_DIM), 1.0),
        'cache_dsa_kv': nrm(ks[5], (n_b, n_pool, PAGE_SIZE, 2, N_KV, HEAD_DIM), 1.0),
        'cache_dsa_idx': nrm(ks[6], (n_b, n_pool, PAGE_SIZE, IDX_DIM), 1.0),
        'page_table': page_table,
        'norm_g': 1.0 + nrm(ks[8], (DEPTH, 3, D_MODEL), 0.01),
        'w_ffn_in': nrm(ks[9], (DEPTH, 2, D_MODEL, 2 * D_FF), D_MODEL ** -0.5),
        'w_ffn_out': nrm(ks[10], (DEPTH, 2, D_FF, D_MODEL), D_FF ** -0.5),
        'w_nsa_in': nrm(ks[11], (n_a, D_MODEL, NSA_COLS), D_MODEL ** -0.5),
        'w_nsa_out': nrm(ks[12], (n_a, Q_DIM, D_MODEL), Q_DIM ** -0.5),
        'nsa_cmp_pos': nrm(ks[13], (n_a, BLOCK, 2, HEAD_DIM), 0.1),
        'nsa_cmp_w1': nrm(ks[14], (n_a, 2, BLOCK, HEAD_DIM, CMP_HIDDEN), (BLOCK * HEAD_DIM) ** -0.5),
        'nsa_cmp_w2': nrm(ks[15], (n_a, 2, CMP_HIDDEN, HEAD_DIM), CMP_HIDDEN ** -0.5),
        'w_dsa_in': nrm(ks[16], (n_b, D_MODEL, DSA_COLS), D_MODEL ** -0.5),
        'w_dsa_out': nrm(ks[17], (n_b, Q_DIM, D_MODEL), Q_DIM ** -0.5),
        'final_norm_g': 1.0 + nrm(ks[18], (D_MODEL,), 0.01),
    }


def reference(x_prompt, x_sample, cache_nsa_cmp, cache_nsa_slc, state_nsa_win, cache_dsa_kv, cache_dsa_idx,
              page_table, norm_g, w_ffn_in, w_ffn_out, w_nsa_in, w_nsa_out, nsa_cmp_pos, nsa_cmp_w1,
              nsa_cmp_w2, w_dsa_in, w_dsa_out, final_norm_g):
    xp, xs = x_prompt, x_sample
    cmp_p, cmp_s, slc_p, slc_s, win_p, win_s = [], [], [], [], [], []
    dkv_p, dkv_s, didx_p, didx_s = [], [], [], []
    for layer in range(DEPTH):
        kind, j = layer % N_MIXERS, layer // N_MIXERS
        xp = xp + 0.5 * swiglu(rmsnorm(xp, norm_g[layer, 0]), w_ffn_in[layer, 0], w_ffn_out[layer, 0])
        xs = xs + 0.5 * swiglu(rmsnorm(xs, norm_g[layer, 0]), w_ffn_in[layer, 0], w_ffn_out[layer, 0])
        hp = rmsnorm(xp, norm_g[layer, 1])
        hs = rmsnorm(xs, norm_g[layer, 1])
        if kind == 0:
            yp, c_p, s_p, w_p = nsa_prompt(hp, w_nsa_in[j], w_nsa_out[j], nsa_cmp_pos[j], nsa_cmp_w1[j], nsa_cmp_w2[j])
            ys, c_s, s_s, w_s = nsa_sample(hs, cache_nsa_cmp[j], cache_nsa_slc[j], state_nsa_win[j], page_table,
                                           w_nsa_in[j], w_nsa_out[j], nsa_cmp_pos[j], nsa_cmp_w1[j], nsa_cmp_w2[j])
            cmp_p.append(c_p); cmp_s.append(c_s); slc_p.append(s_p); slc_s.append(s_s)
            win_p.append(w_p); win_s.append(w_s)
        else:
            yp, kv_p, ki_p = dsa_prompt(hp, w_dsa_in[j], w_dsa_out[j])
            ys, kv_s, ki_s = dsa_sample(hs, cache_dsa_kv[j], cache_dsa_idx[j], page_table, w_dsa_in[j], w_dsa_out[j])
            dkv_p.append(kv_p); dkv_s.append(kv_s); didx_p.append(ki_p); didx_s.append(ki_s)
        xp = xp + yp
        xs = xs + ys
        xp = xp + 0.5 * swiglu(rmsnorm(xp, norm_g[layer, 2]), w_ffn_in[layer, 1], w_ffn_out[layer, 1])
        xs = xs + 0.5 * swiglu(rmsnorm(xs, norm_g[layer, 2]), w_ffn_in[layer, 1], w_ffn_out[layer, 1])
    y_prompt = rmsnorm(xp, final_norm_g)
    y_sample = rmsnorm(xs, final_norm_g)
    return (y_prompt, y_sample, jnp.stack(cmp_p), jnp.stack(cmp_s), jnp.stack(slc_p), jnp.stack(slc_s),
            jnp.stack(win_p), jnp.stack(win_s), jnp.stack(dkv_p), jnp.stack(dkv_s),
            jnp.stack(didx_p), jnp.stack(didx_s))
```

```python
import functools

import jax
import jax.numpy as jnp
from jax import lax
from jax.experimental import pallas as pl
from jax.experimental.pallas import tpu as pltpu

F32 = jnp.float32
BF16 = jnp.bfloat16
I32 = jnp.int32

D_MODEL = 1024
N_HEADS = 16
HEAD_DIM = 64
N_KV = 4
GROUP = N_HEADS // N_KV
Q_DIM = N_HEADS * HEAD_DIM
KV_DIM = N_KV * HEAD_DIM
ROT_DIM = HEAD_DIM // 4
ROPE_THETA = 500000.0
D_FF = 2816
BLOCK = 64
N_SEL = 8
WINDOW = 512
FORCE = 1.0e4
N_IDX = 8
IDX_DIM = 64
DSA_TOPK = 256
PAGE_SIZE = 128
RMS_EPS = 1e-6
NEG = -1e30

LANES = 128
KV_COLS = 2 * KV_DIM
Q_SCALE = HEAD_DIM ** -0.5
IDX_SCALE = IDX_DIM ** -0.5
INT_MIN = -(2 ** 31)
VMEM_LIMIT = 56 * 2 ** 20


def _params(sem):
    return pltpu.CompilerParams(dimension_semantics=sem, vmem_limit_bytes=VMEM_LIMIT)


def _dot(a, b):
    return jnp.dot(a, b, preferred_element_type=F32)


def _dot_nt(a, b):
    return lax.dot_general(a, b, (((1,), (1,)), ((), ())), preferred_element_type=F32)


def _iota(shape, dim):
    return lax.broadcasted_iota(I32, shape, dim)


def _rms(x, g):
    return x * lax.rsqrt(jnp.mean(x * x, axis=-1, keepdims=True) + RMS_EPS) * g


def _rope(x, c, s1, s2):
    half = ROT_DIM // 2
    outs = []
    for j in range(x.shape[1] // LANES):
        xs = x[:, j * LANES:(j + 1) * LANES]
        outs.append(xs * c + pltpu.roll(xs, LANES - half, 1) * s1 + pltpu.roll(xs, half, 1) * s2)
    return outs[0] if len(outs) == 1 else jnp.concatenate(outs, axis=1)


def _softmax_masked(s, mask):
    s = jnp.where(mask, s, NEG)
    p = jnp.where(mask, jnp.exp(s - jnp.max(s, axis=-1, keepdims=True)), 0.0)
    return p / jnp.maximum(jnp.sum(p, axis=-1, keepdims=True), 1e-30)


def _tile_rows(x, n):
    return jnp.concatenate([x] * n, axis=0)


def _ffn_body(*refs, n_ff, final):
    if final:
        x_ref, g_ref, wg_ref, wu_ref, wo_ref, fg_ref, o_ref, h_scr, acc_scr = refs
    else:
        x_ref, g_ref, wg_ref, wu_ref, wo_ref, o_ref, h_scr, acc_scr = refs
    j = pl.program_id(1)

    @pl.when(j == 0)
    def _():
        h_scr[...] = _rms(x_ref[...], g_ref[...]).astype(BF16)
        acc_scr[...] = jnp.zeros_like(acc_scr)

    h = h_scr[...]
    gate = _dot(h, wg_ref[...])
    up = _dot(h, wu_ref[...])
    act = (gate * jax.nn.sigmoid(gate) * up).astype(BF16)
    acc_scr[...] += _dot(act, wo_ref[...])

    @pl.when(j == n_ff - 1)
    def _():
        y = x_ref[...] + 0.5 * acc_scr[...]
        if final:
            y = _rms(y, fg_ref[...])
        o_ref[...] = y


def _ffn(x, g, w_in, w_out, final_g=None):
    n, d = x.shape
    tm = min(512, n)
    n_ff = 2
    fc = D_FF // n_ff
    in_specs = [
        pl.BlockSpec((tm, d), lambda i, j: (i, 0)),
        pl.BlockSpec((1, d), lambda i, j: (0, 0)),
        pl.BlockSpec((d, fc), lambda i, j: (0, j)),
        pl.BlockSpec((d, fc), lambda i, j: (0, j + n_ff)),
        pl.BlockSpec((fc, d), lambda i, j: (j, 0)),
    ]
    args = [x, g.reshape(1, d), w_in, w_in, w_out]
    if final_g is not None:
        in_specs.append(pl.BlockSpec((1, d), lambda i, j: (0, 0)))
        args.append(final_g.reshape(1, d))
    return pl.pallas_call(
        functools.partial(_ffn_body, n_ff=n_ff, final=final_g is not None),
        grid=(n // tm, n_ff),
        in_specs=in_specs,
        out_specs=pl.BlockSpec((tm, d), lambda i, j: (i, 0)),
        out_shape=jax.ShapeDtypeStruct((n, d), F32),
        scratch_shapes=[pltpu.VMEM((tm, d), BF16), pltpu.VMEM((tm, d), F32)],
        compiler_params=_params(("parallel", "arbitrary")),
        name="ffn",
    )(*args)


def _oproj_body(x_ref, o_ref, w_ref, y_ref):
    y_ref[...] = x_ref[...] + _dot(o_ref[...], w_ref[...])


def _oproj(x, o, w):
    n, d = x.shape
    tm = min(512, n)
    return pl.pallas_call(
        _oproj_body,
        grid=(n // tm,),
        in_specs=[pl.BlockSpec((tm, d), lambda i: (i, 0)),
                  pl.BlockSpec((tm, Q_DIM), lambda i: (i, 0)),
                  pl.BlockSpec((Q_DIM, d), lambda i: (0, 0))],
        out_specs=pl.BlockSpec((tm, d), lambda i: (i, 0)),
        out_shape=jax.ShapeDtypeStruct((n, d), F32),
        compiler_params=_params(("parallel",)),
        name="oproj",
    )(x, o, w)


def _nsa_proj_body(x_ref, g_ref, wq_ref, wkv_ref, wgt_ref, c_ref, s1_ref, s2_ref,
                   q_ref, qr_ref, gt_ref, cmp_ref, slc_ref, win_ref, slcb_ref, winb_ref):
    h = _rms(x_ref[...], g_ref[...]).astype(BF16)
    c, s1, s2 = c_ref[...], s1_ref[...], s2_ref[...]
    q = _dot(h, wq_ref[...])
    q_ref[...] = (q * Q_SCALE).astype(BF16)
    qr_ref[...] = (_rope(q, c, s1, s2) * Q_SCALE).astype(BF16)
    gt_ref[...] = jax.nn.sigmoid(_dot(h, wgt_ref[...]))
    kv = _dot(h, wkv_ref[...])
    cmp_ref[...] = kv[:, :KV_COLS]
    for j, (f_ref, b_ref) in enumerate(((slc_ref, slcb_ref), (win_ref, winb_ref))):
        base = (j + 1) * KV_COLS
        kr = _rope(kv[:, base:base + KV_DIM], c, s1, s2)
        row = jnp.concatenate([kr, kv[:, base + KV_DIM:base + KV_COLS]], axis=1)
        f_ref[...] = row
        b_ref[...] = row.astype(BF16)


def _nsa_proj(x, g, wq, wkv, wgt, tabs, n_tab_blocks):
    n, d = x.shape
    tm = min(512, n)
    row = lambda w: pl.BlockSpec((tm, w), lambda i: (i, 0))
    full = lambda a: pl.BlockSpec(a.shape, lambda i: (0, 0))
    tab = pl.BlockSpec((tm, LANES), lambda i: (i % n_tab_blocks, 0))
    outs = [(Q_DIM, BF16), (Q_DIM, BF16), (2 * LANES, F32), (KV_COLS, F32), (KV_COLS, F32),
            (KV_COLS, F32), (KV_COLS, BF16), (KV_COLS, BF16)]
    return pl.pallas_call(
        _nsa_proj_body,
        grid=(n // tm,),
        in_specs=[row(d), pl.BlockSpec((1, d), lambda i: (0, 0)), full(wq), full(wkv), full(wgt),
                  tab, tab, tab],
        out_specs=[row(w) for w, _ in outs],
        out_shape=[jax.ShapeDtypeStruct((n, w), t) for w, t in outs],
        compiler_params=_params(("parallel",)),
        name="nsa_proj",
    )(x, g.reshape(1, d), wq, wkv, wgt, *tabs)


def _dsa_proj_body(x_ref, g_ref, wq_ref, wkv_ref, wqi_ref, wki_ref, wwi_ref, c_ref, s1_ref, s2_ref,
                   q_ref, kv_ref, kvb_ref, qi_ref, ki_ref, kib_ref, wi_ref):
    h = _rms(x_ref[...], g_ref[...]).astype(BF16)
    c, s1, s2 = c_ref[...], s1_ref[...], s2_ref[...]
    q_ref[...] = (_rope(_dot(h, wq_ref[...]), c, s1, s2) * Q_SCALE).astype(BF16)
    kv = _dot(h, wkv_ref[...])
    row = jnp.concatenate([_rope(kv[:, :KV_DIM], c, s1, s2), kv[:, KV_DIM:]], axis=1)
    kv_ref[...] = row
    kvb_ref[...] = row.astype(BF16)
    qi_ref[...] = (_rope(_dot(h, wqi_ref[...]), c, s1, s2) * IDX_SCALE).astype(BF16)
    ki = _rope(_dot(h, wki_ref[...]), c, s1, s2)[:, :IDX_DIM]
    ki_ref[...] = ki
    kib_ref[...] = ki.astype(BF16)
    wi_ref[...] = _dot(h, wwi_ref[...]) * N_IDX ** -0.5


def _dsa_proj(x, g, wq, wkv, wqi, wki, wwi, tabs, n_tab_blocks):
    n, d = x.shape
    tm = min(512, n)
    row = lambda w: pl.BlockSpec((tm, w), lambda i: (i, 0))
    full = lambda a: pl.BlockSpec(a.shape, lambda i: (0, 0))
    tab = pl.BlockSpec((tm, LANES), lambda i: (i % n_tab_blocks, 0))
    outs = [(Q_DIM, BF16), (KV_COLS, F32), (KV_COLS, BF16), (N_IDX * LANES, BF16),
            (IDX_DIM, F32), (IDX_DIM, BF16), (LANES, F32)]
    return pl.pallas_call(
        _dsa_proj_body,
        grid=(n // tm,),
        in_specs=[row(d), pl.BlockSpec((1, d), lambda i: (0, 0)), full(wq), full(wkv), full(wqi),
                  full(wki), full(wwi), tab, tab, tab],
        out_specs=[row(w) for w, _ in outs],
        out_shape=[jax.ShapeDtypeStruct((n, w), t) for w, t in outs],
        compiler_params=_params(("parallel",)),
        name="dsa_proj",
    )(x, g.reshape(1, d), wq, wkv, wqi, wki, wwi, *tabs)


def _compress_body(*refs, n_in, nblk, paged):
    if paged:
        refs = refs[1:]
    x_refs = refs[:n_in]
    pos_ref, w1_ref, w2_ref, o_ref = refs[n_in:n_in + 4]
    x_scr = refs[n_in + 4]
    rows = x_refs[0].shape[1]
    for p in range(n_in):
        for j in range(KV_COLS // LANES):
            x_scr[j, p * rows:(p + 1) * rows, :] = x_refs[p][0, :, j * LANES:(j + 1) * LANES]
    load = lambda l, col: x_scr[col // LANES, pl.ds(l, nblk, stride=BLOCK), :]
    outs = []
    for c in range(2):
        acc = jnp.zeros((N_KV * nblk, w1_ref.shape[-1]), F32)
        for l in range(BLOCK):
            pieces = []
            for kp in range(N_KV // 2):
                col = c * KV_DIM + kp * LANES
                a = (load(l, col) + pos_ref[l:l + 1, col:col + LANES]).astype(BF16)
                pieces += [a[:, :HEAD_DIM], a[:, HEAD_DIM:]]
            acc = acc + _dot(jnp.concatenate(pieces, axis=0), w1_ref[c, l])
        out = _dot(jax.nn.gelu(acc).astype(BF16), w2_ref[c])
        outs += [out[k * nblk:(k + 1) * nblk] for k in range(N_KV)]
    o_ref[...] = jnp.concatenate(outs, axis=1).astype(BF16)


def _compress_rows(x3, pos, w1, w2):
    g, r, _ = x3.shape
    nblk = r // BLOCK
    full = lambda a: pl.BlockSpec(a.shape, lambda i: (0,) * a.ndim)
    return pl.pallas_call(
        functools.partial(_compress_body, n_in=1, nblk=nblk, paged=False),
        grid=(g,),
        in_specs=[pl.BlockSpec((1, r, KV_COLS), lambda i: (i, 0, 0)), full(pos), full(w1), full(w2)],
        out_specs=pl.BlockSpec((nblk, KV_COLS), lambda i: (i, 0)),
        out_shape=jax.ShapeDtypeStruct((g * nblk, KV_COLS), BF16),
        scratch_shapes=[pltpu.VMEM((KV_COLS // LANES, r, LANES), F32)],
        compiler_params=_params(("parallel",)),
        name="compress_rows",
    )(x3, pos, w1, w2)


def _compress_paged(pool, page_table, pos, w1, w2):
    db, n_pages = page_table.shape
    per_step = 2
    n_in = per_step * n_pages
    nblk = n_in * PAGE_SIZE // BLOCK
    full = lambda a: pl.BlockSpec(a.shape, lambda i, pt: (0,) * a.ndim)
    page = lambda p: pl.BlockSpec(
        (1, PAGE_SIZE, KV_COLS),
        lambda i, pt: (pt[per_step * i + p // n_pages, p % n_pages], 0, 0))
    return pl.pallas_call(
        functools.partial(_compress_body, n_in=n_in, nblk=nblk, paged=True),
        grid_spec=pltpu.PrefetchScalarGridSpec(
            num_scalar_prefetch=1,
            grid=(db // per_step,),
            in_specs=[page(p) for p in range(n_in)] + [full(pos), full(w1), full(w2)],
            out_specs=pl.BlockSpec((nblk, KV_COLS), lambda i, pt: (i, 0)),
            scratch_shapes=[pltpu.VMEM((KV_COLS // LANES, n_in * PAGE_SIZE, LANES), F32)]),
        out_shape=jax.ShapeDtypeStruct((db // per_step * nblk, KV_COLS), BF16),
        compiler_params=_params(("parallel",)),
        name="compress_paged",
    )(page_table, *([pool] * n_in), pos, w1, w2)


def _pair_rows(blk, half, tq):
    lane = _iota((1, LANES), 1)
    keep = (lane < HEAD_DIM) if half == 0 else (lane >= HEAD_DIM)
    zero = jnp.zeros((tq, LANES), blk.dtype)
    return jnp.concatenate(
        [jnp.where(keep, blk[:, g * LANES:(g + 1) * LANES], zero) for g in range(GROUP)], axis=0)


def _select_top(score, n):
    idx = _iota(score.shape, 1).astype(F32)
    sel = jnp.zeros(score.shape, F32)
    for _ in range(n):
        m = jnp.max(score, axis=-1, keepdims=True)
        j = jnp.min(jnp.where(score == m, idx, 1e9), axis=-1, keepdims=True)
        hit = idx == j
        sel = jnp.where(hit, jnp.where(m >= 0.0, 1.0, 0.0), sel)
        score = jnp.where(hit, -2.0, score)
    return sel


def _online_attend(q_rows, get_k, get_v, get_mask, n_chunks):
    r = q_rows.shape[0]

    def body(c, carry):
        m, l, acc = carry
        mask = get_mask(c)
        s = jnp.where(mask, _dot_nt(q_rows, get_k(c)), NEG)
        m_new = jnp.maximum(m, jnp.max(s, axis=-1, keepdims=True))
        alpha = jnp.exp(m - m_new)
        p = jnp.where(mask, jnp.exp(s - m_new), 0.0)
        l = alpha * l + jnp.sum(p, axis=-1, keepdims=True)
        acc = alpha * acc + _dot(p.astype(BF16), get_v(c))
        return m_new, l, acc

    init = (jnp.full((r, 1), NEG, F32), jnp.zeros((r, 1), F32), jnp.zeros((r, LANES), F32))
    if isinstance(n_chunks, int):
        carry = init
        for c in range(n_chunks):
            carry = body(c, carry)
        _, l, acc = carry
    else:
        _, l, acc = lax.fori_loop(0, n_chunks, body, init)
    return acc / jnp.maximum(l, 1e-30)


def _nsa_core(q_blk, qr_blk, gates, kc, vc, get_k, get_v, n_chunks, ck, wk, wv, wstart, qpos_t):
    tq = q_blk.shape[0]
    nb = kc.shape[0]
    wlen = wk.shape[0]
    lane = _iota((1, LANES), 1)
    qpos_r = _tile_rows(qpos_t, GROUP)
    blk = _iota((1, nb), 1)
    cur = jnp.right_shift(qpos_t, 6)
    per_half = []
    for half in range(2):
        q_rows = _pair_rows(q_blk, half, tq)
        qr_rows = _pair_rows(qr_blk, half, tq)
        cmask = ((blk + 1) * BLOCK - 1) <= qpos_r
        p_cmp = _softmax_masked(_dot_nt(q_rows, kc), cmask)
        o_cmp = _dot(p_cmp.astype(BF16), vc)
        imp = p_cmp[0:tq]
        for g in range(1, GROUP):
            imp = imp + p_cmp[g * tq:(g + 1) * tq]
        forced = (blk == 0) | (blk == cur) | (blk == cur - 1)
        score = jnp.where(forced, FORCE, imp)
        score = jnp.where(blk <= cur, score, -1.0)
        sel = _select_top(score, min(N_SEL, nb)).astype(BF16)

        def slc_mask(c, sel=sel):
            kpos = c * ck + _iota((1, ck), 1)
            expand = jnp.where(jnp.right_shift(kpos, 6) == _iota((nb, 1), 0), 1.0, 0.0).astype(BF16)
            chosen = _dot(sel, expand)
            mk = jnp.where((chosen > 0.5) & (kpos <= qpos_t), 1.0, 0.0)
            return _tile_rows(mk, GROUP) > 0.5

        o_slc = _online_attend(qr_rows, get_k, get_v, slc_mask, n_chunks)
        kwpos = wstart + _iota((1, wlen), 1)
        dpos = qpos_r - kwpos
        wmask = (dpos >= 0) & (dpos <= WINDOW) & (kwpos >= 0)
        o_win = _dot(_softmax_masked(_dot_nt(qr_rows, wk), wmask).astype(BF16), wv)
        outs = []
        for g in range(GROUP):
            c0 = half * GROUP * 3 + g * 3
            sl = slice(g * tq, (g + 1) * tq)
            outs.append(gates[:, c0:c0 + 1] * o_cmp[sl] + gates[:, c0 + 1:c0 + 2] * o_slc[sl]
                        + gates[:, c0 + 2:c0 + 3] * o_win[sl])
        per_half.append(outs)
    return jnp.concatenate(
        [jnp.where(lane < HEAD_DIM, per_half[0][g], per_half[1][g]) for g in range(GROUP)], axis=1)


def _nsa_prompt_body(q_ref, qr_ref, gt_ref, kc_ref, vc_ref, sk_ref, sv_ref, wk_ref, wv_ref, o_ref,
                     *, tq, ck):
    s0 = pl.program_id(2) * tq
    qpos_t = s0 + _iota((tq, 1), 0)
    n_chunks = (s0 + tq + ck - 1) // ck
    wstart = pl.multiple_of(jnp.maximum(s0 - WINDOW, 0), tq)
    chunk = lambda ref: (lambda c: ref[pl.ds(pl.multiple_of(c * ck, ck), ck), :])
    out = _nsa_core(q_ref[...], qr_ref[...], gt_ref[...], kc_ref[...], vc_ref[...],
                    chunk(sk_ref), chunk(sv_ref), n_chunks, ck,
                    wk_ref[pl.ds(wstart, WINDOW + tq), :], wv_ref[pl.ds(wstart, WINDOW + tq), :],
                    wstart, qpos_t)
    o_ref[...] = out.astype(BF16)


def _nsa_attn_prompt(q, qr, gates, kvc, slc_b, win_b, b, t):
    tq = 128
    ck = min(512, t)
    nt = t // tq
    nb = t // BLOCK
    qspec = pl.BlockSpec((tq, 4 * LANES), lambda bi, kp, i: (bi * nt + i, kp))
    seq = lambda off: pl.BlockSpec((t, LANES), lambda bi, kp, i: (bi, kp + off))
    return pl.pallas_call(
        functools.partial(_nsa_prompt_body, tq=tq, ck=ck),
        grid=(b, 2, nt),
        in_specs=[qspec, qspec,
                  pl.BlockSpec((tq, LANES), lambda bi, kp, i: (bi * nt + i, kp)),
                  pl.BlockSpec((nb, LANES), lambda bi, kp, i: (bi, kp)),
                  pl.BlockSpec((nb, LANES), lambda bi, kp, i: (bi, kp + 2)),
                  seq(0), seq(2), seq(0), seq(2)],
        out_specs=qspec,
        out_shape=jax.ShapeDtypeStruct((b * t, Q_DIM), BF16),
        compiler_params=_params(("parallel", "parallel", "arbitrary")),
        name="nsa_attn_prompt",
    )(q, qr, gates, kvc, kvc, slc_b, slc_b, win_b, win_b)


def _tail_rows(row, n, dtype):
    w = row.shape[-1]
    first = _iota((n, w), 0) == 0
    return jnp.where(first, jnp.broadcast_to(row, (n, w)), 0.0).astype(dtype)


def _nsa_sample_body(pt_ref, q_ref, qr_ref, gt_ref, kcp_ref, kcn_ref, *rest,
                     n_pages, n_past, wb, tq, tail):
    del pt_ref
    pages = rest[:n_pages]
    slcn_ref, wst_ref, winn_ref, o_ref, kc_scr, slc_scr, win_scr = rest[n_pages:]
    nbp = kcp_ref.shape[0]
    kc_scr[...] = jnp.zeros_like(kc_scr)
    kc_scr[0:nbp, :] = kcp_ref[...]
    kc_scr[nbp:nbp + 16, :] = _tail_rows(kcn_ref[0].astype(F32), 16, BF16)
    for p in range(n_pages):
        slc_scr[p * PAGE_SIZE:(p + 1) * PAGE_SIZE, :] = pages[p][0].astype(BF16)
    slc_scr[n_past:n_past + tail, :] = _tail_rows(slcn_ref[0], tail, BF16)
    win_scr[0:wb, :] = wst_ref[0].astype(BF16)
    win_scr[wb:wb + 16, :] = _tail_rows(winn_ref[0], 16, BF16)
    qpos_t = jnp.full((tq, 1), n_past, I32)
    ck = n_past + tail
    outs = []
    for kp in range(2):
        cols = slice(kp * 4 * LANES, (kp + 1) * 4 * LANES)
        kcol = slice(kp * LANES, (kp + 1) * LANES)
        vcol = slice((kp + 2) * LANES, (kp + 3) * LANES)
        outs.append(_nsa_core(
            jnp.broadcast_to(q_ref[0][:, cols], (tq, 4 * LANES)),
            jnp.broadcast_to(qr_ref[0][:, cols], (tq, 4 * LANES)),
            jnp.broadcast_to(gt_ref[0][:, kcol], (tq, LANES)),
            kc_scr[:, kcol], kc_scr[:, vcol],
            lambda c, kcol=kcol: slc_scr[:, kcol], lambda c, vcol=vcol: slc_scr[:, vcol], 1, ck,
            win_scr[:, kcol], win_scr[:, vcol], n_past - wb, qpos_t))
    o_ref[0] = jnp.concatenate(outs, axis=1)[0:1].astype(BF16)


def _nsa_attn_sample(q, qr, gates, kvc_past, kvc_new, cache_slc, page_table, slc_new, win_state,
                     win_new):
    db, n_pages = page_table.shape
    n_past = n_pages * PAGE_SIZE
    wb = win_state.shape[1]
    nbp = n_past // BLOCK
    tq, tail = 8, PAGE_SIZE
    nb = 64
    assert nb * BLOCK >= n_past + tail and nbp + 16 <= nb
    one = lambda w: pl.BlockSpec((1, 1, w), lambda i, pt: (i, 0, 0))
    page = lambda p: pl.BlockSpec((1, PAGE_SIZE, KV_COLS), lambda i, pt: (pt[i, p], 0, 0))
    r3 = lambda a: a.reshape(db, 1, a.shape[-1])
    return pl.pallas_call(
        functools.partial(_nsa_sample_body, n_pages=n_pages, n_past=n_past, wb=wb, tq=tq, tail=tail),
        grid_spec=pltpu.PrefetchScalarGridSpec(
            num_scalar_prefetch=1,
            grid=(db,),
            in_specs=[one(Q_DIM), one(Q_DIM), one(2 * LANES),
                      pl.BlockSpec((nbp, KV_COLS), lambda i, pt: (i, 0)), one(KV_COLS)]
                     + [page(p) for p in range(n_pages)]
                     + [one(KV_COLS), pl.BlockSpec((1, wb, KV_COLS), lambda i, pt: (i, 0, 0)),
                        one(KV_COLS)],
            out_specs=one(Q_DIM),
            scratch_shapes=[pltpu.VMEM((nb, KV_COLS), BF16),
                            pltpu.VMEM((n_past + tail, KV_COLS), BF16),
                            pltpu.VMEM((wb + 16, KV_COLS), BF16)]),
        out_shape=jax.ShapeDtypeStruct((db, 1, Q_DIM), BF16),
        compiler_params=_params(("parallel",)),
        name="nsa_attn_sample",
    )(page_table, r3(q), r3(qr), r3(gates), kvc_past, r3(kvc_new), *([cache_slc] * n_pages),
      r3(slc_new), win_state, r3(win_new)).reshape(db, Q_DIM)


def _dsa_core(q_blk, qi_blk, wi, get_ki, get_k, get_v, n_chunks, ck, qpos_t, key_scr, topk):
    tq = q_blk.shape[0]
    qi_rows = jnp.concatenate(
        [qi_blk[:, h * LANES:h * LANES + IDX_DIM] for h in range(N_IDX)], axis=0)
    wcol = [wi[:, h:h + 1] for h in range(N_IDX)]

    def loop(fn, init):
        if isinstance(n_chunks, int):
            for c in range(n_chunks):
                init = fn(c, init)
            return init
        return lax.fori_loop(0, n_chunks, fn, init)

    def kpos_of(c):
        return c * ck + _iota((1, ck), 1)

    def score_chunk(c, carry):
        sc = _dot_nt(qi_rows, get_ki(c))
        score = jnp.maximum(sc[0:tq], 0.0) * wcol[0]
        for h in range(1, N_IDX):
            score = score + jnp.maximum(sc[h * tq:(h + 1) * tq], 0.0) * wcol[h]
        score = jnp.where(kpos_of(c) <= qpos_t, score, NEG)
        score = jnp.where(score == 0.0, 0.0, score)
        bits = lax.bitcast_convert_type(score, I32)
        key_scr[c] = jnp.where(bits < 0, bits ^ 0x7FFFFFFF, bits)
        return carry

    loop(score_chunk, 0)

    def count(pred):
        def fn(c, acc):
            hit = jnp.where(pred(key_scr[c], kpos_of(c)), 1.0, 0.0)
            part = hit[:, 0:LANES]
            for j in range(1, ck // LANES):
                part = part + hit[:, j * LANES:(j + 1) * LANES]
            return acc + part
        return jnp.sum(loop(fn, jnp.zeros((tq, LANES), F32)), axis=-1, keepdims=True)

    kf = float(topk)
    lo = jnp.where(count(lambda k, p: k >= 0) >= kf, 0, INT_MIN).astype(I32)

    def bit_step(it, lo):
        cand = lo + jnp.left_shift(jnp.int32(1), 30 - it)
        return jnp.where(count(lambda k, p: k >= cand) >= kf, cand, lo)

    thr = lax.fori_loop(0, 31, bit_step, lo)
    need = kf - count(lambda k, p: k > thr)

    def idx_step(it, lo):
        mid = lo + jnp.left_shift(jnp.int32(1), 13 - it)
        enough = count(lambda k, p: (k == thr) & (p <= mid)) >= need
        return jnp.where(enough, lo, mid)

    jmax = lax.fori_loop(0, 14, idx_step, jnp.full((tq, 1), -1, I32)) + 1

    def mask_chunk(c, carry):
        k = key_scr[c]
        p = kpos_of(c)
        keep = ((k > thr) | ((k == thr) & (p <= jmax))) & (p <= qpos_t)
        key_scr[c] = jnp.where(keep, 1, 0)
        return carry

    loop(mask_chunk, 0)

    lane = _iota((1, LANES), 1)
    chunks = []
    for kp in range(2):
        per_half = []
        for half in range(2):
            q_rows = _pair_rows(q_blk[:, kp * 4 * LANES:(kp + 1) * 4 * LANES], half, tq)
            o = _online_attend(q_rows, lambda c, kp=kp: get_k(kp, c), lambda c, kp=kp: get_v(kp, c),
                               lambda c: _tile_rows(key_scr[c], GROUP) > 0, n_chunks)
            per_half.append(o)
        for g in range(GROUP):
            sl = slice(g * tq, (g + 1) * tq)
            chunks.append(jnp.where(lane < HEAD_DIM, per_half[0][sl], per_half[1][sl]))
    return jnp.concatenate(chunks, axis=1)


def _dsa_prompt_body(q_ref, qi_ref, wi_ref, ki_ref, k0_ref, k1_ref, v0_ref, v1_ref, o_ref, key_scr,
                     *, tq, ck, topk):
    s0 = pl.program_id(1) * tq
    qpos_t = s0 + _iota((tq, 1), 0)
    n_chunks = (s0 + tq + ck - 1) // ck
    rows = lambda c: pl.ds(pl.multiple_of(c * ck, ck), ck)
    k_refs, v_refs = (k0_ref, k1_ref), (v0_ref, v1_ref)
    out = _dsa_core(q_ref[...], qi_ref[...], wi_ref[...], lambda c: ki_ref[rows(c), :],
                    lambda kp, c: k_refs[kp][rows(c), :], lambda kp, c: v_refs[kp][rows(c), :],
                    n_chunks, ck, qpos_t, key_scr, topk)
    o_ref[...] = out.astype(BF16)


def _dsa_attn_prompt(q, qi, wi, ki_b, kv_b, b, t):
    tq = 128
    ck = min(512, t)
    nt = t // tq
    row = lambda w: pl.BlockSpec((tq, w), lambda bi, i: (bi * nt + i, 0))
    seq = lambda j: pl.BlockSpec((t, LANES), lambda bi, i: (bi, j))
    return pl.pallas_call(
        functools.partial(_dsa_prompt_body, tq=tq, ck=ck, topk=min(DSA_TOPK, t // 4)),
        grid=(b, nt),
        in_specs=[row(Q_DIM), row(N_IDX * LANES), row(LANES),
                  pl.BlockSpec((t, IDX_DIM), lambda bi, i: (bi, 0)),
                  seq(0), seq(1), seq(2), seq(3)],
        out_specs=row(Q_DIM),
        out_shape=jax.ShapeDtypeStruct((b * t, Q_DIM), BF16),
        scratch_shapes=[pltpu.VMEM((t // ck, tq, ck), I32)],
        compiler_params=_params(("parallel", "arbitrary")),
        name="dsa_attn_prompt",
    )(q, qi, wi, ki_b, kv_b, kv_b, kv_b, kv_b)


def _dsa_sample_body(pt_ref, q_ref, qi_ref, wi_ref, *rest, n_pages, n_past, tq, tail, topk):
    del pt_ref
    kv_pages = rest[:n_pages]
    ki_pages = rest[n_pages:2 * n_pages]
    kvn_ref, kin_ref, o_ref, kv_scr, ki_scr, key_scr = rest[2 * n_pages:]
    for p in range(n_pages):
        rows = slice(p * PAGE_SIZE, (p + 1) * PAGE_SIZE)
        kv_scr[rows, :] = kv_pages[p][0].astype(BF16)
        ki_scr[rows, :] = ki_pages[p][0].astype(BF16)
    kv_scr[n_past:n_past + tail, :] = _tail_rows(kvn_ref[0], tail, BF16)
    ki_scr[n_past:n_past + tail, :] = _tail_rows(kin_ref[0], tail, BF16)
    qpos_t = jnp.full((tq, 1), n_past, I32)
    ck = n_past + tail
    out = _dsa_core(jnp.broadcast_to(q_ref[0], (tq, Q_DIM)),
                    jnp.broadcast_to(qi_ref[0], (tq, N_IDX * LANES)),
                    jnp.broadcast_to(wi_ref[0], (tq, LANES)),
                    lambda c: ki_scr[...],
                    lambda kp, c: kv_scr[:, kp * LANES:(kp + 1) * LANES],
                    lambda kp, c: kv_scr[:, (kp + 2) * LANES:(kp + 3) * LANES],
                    1, ck, qpos_t, key_scr, topk)
    o_ref[0] = out[0:1].astype(BF16)


def _dsa_attn_sample(q, qi, wi, cache_kv, cache_idx, page_table, kv_new, ki_new):
    db, n_pages = page_table.shape
    n_past = n_pages * PAGE_SIZE
    tq, tail = 8, PAGE_SIZE
    one = lambda w: pl.BlockSpec((1, 1, w), lambda i, pt: (i, 0, 0))
    page = lambda p, w: pl.BlockSpec((1, PAGE_SIZE, w), lambda i, pt: (pt[i, p], 0, 0))
    r3 = lambda a: a.reshape(db, 1, a.shape[-1])
    return pl.pallas_call(
        functools.partial(_dsa_sample_body, n_pages=n_pages, n_past=n_past, tq=tq, tail=tail,
                          topk=min(DSA_TOPK, (n_past + 1) // 4)),
        grid_spec=pltpu.PrefetchScalarGridSpec(
            num_scalar_prefetch=1,
            grid=(db,),
            in_specs=[one(Q_DIM), one(N_IDX * LANES), one(LANES)]
                     + [page(p, KV_COLS) for p in range(n_pages)]
                     + [page(p, IDX_DIM) for p in range(n_pages)]
                     + [one(KV_COLS), one(IDX_DIM)],
            out_specs=one(Q_DIM),
            scratch_shapes=[pltpu.VMEM((n_past + tail, KV_COLS), BF16),
                            pltpu.VMEM((n_past + tail, IDX_DIM), BF16),
                            pltpu.VMEM((1, tq, n_past + tail), I32)]),
        out_shape=jax.ShapeDtypeStruct((db, 1, Q_DIM), BF16),
        compiler_params=_params(("parallel",)),
        name="dsa_attn_sample",
    )(page_table, r3(q), r3(qi), r3(wi), *([cache_kv] * n_pages), *([cache_idx] * n_pages),
      r3(kv_new), r3(ki_new)).reshape(db, Q_DIM)


def _rope_tables(pos):
    half = ROT_DIM // 2
    n = pos.shape[0]
    inv = ROPE_THETA ** (-jnp.arange(half, dtype=F32) * 2.0 / ROT_DIM)
    ang = pos.astype(F32)[:, None] * inv[None, :]
    cos, sin = jnp.cos(ang), jnp.sin(ang)
    z8 = jnp.zeros((n, half), F32)
    rest = HEAD_DIM - ROT_DIM
    c = jnp.concatenate([cos, cos, jnp.ones((n, rest), F32)], axis=1)
    s1 = jnp.concatenate([-sin, z8, jnp.zeros((n, rest), F32)], axis=1)
    s2 = jnp.concatenate([z8, sin, jnp.zeros((n, rest), F32)], axis=1)
    rep = LANES // HEAD_DIM
    return tuple(jnp.tile(a, (1, rep)) for a in (c, s1, s2))


def _pair_perm():
    cols = []
    for kp in range(N_KV // 2):
        for g in range(GROUP):
            for half in range(2):
                h = (2 * kp + half) * GROUP + g
                cols.extend(range(h * HEAD_DIM, (h + 1) * HEAD_DIM))
    return jnp.array(cols, dtype=I32)


def _pad_cols(w, n):
    return jnp.pad(w, ((0, 0), (0, n - w.shape[1])))


def _nsa_layer(xp, xs, g, w_in, w_out, cmp_pos, cmp_w1, cmp_w2, cache_cmp, cache_slc, win_state,
               page_table, tabs_p, tabs_s, b, t):
    db = xs.shape[0]
    perm = _pair_perm()
    wq = w_in[:, :Q_DIM][:, perm].astype(BF16)
    wkv = w_in[:, Q_DIM:Q_DIM + 3 * KV_COLS].astype(BF16)
    wg = w_in[:, Q_DIM + 3 * KV_COLS:]
    wgt = jnp.concatenate([_pad_cols(wg[:, kp * 2 * GROUP * 3:(kp + 1) * 2 * GROUP * 3], LANES)
                           for kp in range(N_KV // 2)], axis=1).astype(BF16)
    wo = w_out[perm, :].astype(BF16)
    pos = jnp.concatenate([jnp.tile(cmp_pos[:, c, :], (1, N_KV)) for c in range(2)], axis=1)
    w1 = cmp_w1.astype(BF16)
    w2 = cmp_w2.astype(BF16)

    q, qr, gates, kcmp, kslc, kwin, slc_b, win_b = _nsa_proj(xp, g, wq, wkv, wgt, tabs_p, t // min(512, t))
    kvc = _compress_rows(kcmp.reshape(b, t, KV_COLS), pos, w1, w2)
    o = _nsa_attn_prompt(q, qr, gates, kvc, slc_b, win_b, b, t)
    xp = _oproj(xp, o, wo)
    n_past = page_table.shape[1] * PAGE_SIZE
    qs, qrs, gs, scmp, sslc, swin, _, _ = _nsa_proj(xs, g, wq, wkv, wgt, tabs_s, 1)
    pool = cache_cmp.reshape(cache_cmp.shape[0], PAGE_SIZE, KV_COLS)
    kvc_past = _compress_paged(pool, page_table, pos, w1, w2)
    new_rows = jnp.pad(scmp[:, None, :], ((0, 0), (0, BLOCK - 1), (0, 0)))
    kvc_new = _compress_rows(new_rows.reshape(2, db // 2 * BLOCK, KV_COLS), pos, w1, w2)
    os_ = _nsa_attn_sample(qs, qrs, gs, kvc_past, kvc_new,
                           cache_slc.reshape(cache_slc.shape[0], PAGE_SIZE, KV_COLS), page_table,
                           sslc, win_state.reshape(db, -1, KV_COLS), swin)
    xs = _oproj(xs, os_, wo)
    wb = min(WINDOW, t)
    kv5 = lambda a, n: a.reshape(n, -1, 2, N_KV, HEAD_DIM)
    win_s = jnp.concatenate([win_state, kv5(swin, db)], axis=1)[:, 1:]
    outs = (kv5(kcmp, b), kv5(scmp, db), kv5(kslc, b), kv5(sslc, db),
            kv5(kwin, b)[:, t - wb:], win_s)
    del n_past
    return xp, xs, outs


def _dsa_layer(xp, xs, g, w_in, w_out, cache_kv, cache_idx, page_table, tabs_p, tabs_s, b, t):
    db = xs.shape[0]
    perm = _pair_perm()
    wq = w_in[:, :Q_DIM][:, perm].astype(BF16)
    o1 = Q_DIM + KV_COLS
    wkv = w_in[:, Q_DIM:o1].astype(BF16)
    o2 = o1 + N_IDX * IDX_DIM
    wqi = jnp.concatenate([_pad_cols(w_in[:, o1 + h * IDX_DIM:o1 + (h + 1) * IDX_DIM], LANES)
                           for h in range(N_IDX)], axis=1).astype(BF16)
    wki = _pad_cols(w_in[:, o2:o2 + IDX_DIM], LANES).astype(BF16)
    wwi = _pad_cols(w_in[:, o2 + IDX_DIM:], LANES).astype(BF16)
    wo = w_out[perm, :].astype(BF16)

    q, kv, kv_b, qi, ki, ki_b, wi = _dsa_proj(xp, g, wq, wkv, wqi, wki, wwi, tabs_p, t // min(512, t))
    o = _dsa_attn_prompt(q, qi, wi, ki_b, kv_b, b, t)
    xp = _oproj(xp, o, wo)
    qs, kvs, _, qis, kis, _, wis = _dsa_proj(xs, g, wq, wkv, wqi, wki, wwi, tabs_s, 1)
    os_ = _dsa_attn_sample(qs, qis, wis, cache_kv.reshape(cache_kv.shape[0], PAGE_SIZE, KV_COLS),
                           cache_idx, page_table, kvs, kis)
    xs = _oproj(xs, os_, wo)
    outs = (kv.reshape(b, t, 2, N_KV, HEAD_DIM), kvs.reshape(db, 1, 2, N_KV, HEAD_DIM),
            ki.reshape(b, t, IDX_DIM), kis.reshape(db, 1, IDX_DIM))
    return xp, xs, outs


def kernel(x_prompt, x_sample, cache_nsa_cmp, cache_nsa_slc, state_nsa_win, cache_dsa_kv, cache_dsa_idx,
           page_table, norm_g, w_ffn_in, w_ffn_out, w_nsa_in, w_nsa_out, nsa_cmp_pos, nsa_cmp_w1,
           nsa_cmp_w2, w_dsa_in, w_dsa_out, final_norm_g):
    b, t, d = x_prompt.shape
    db, ds, _ = x_sample.shape
    assert ds == 1
    depth = norm_g.shape[0]
    n_past = page_table.shape[1] * PAGE_SIZE
    xp = x_prompt.reshape(b * t, d)
    xs = x_sample.reshape(db, d)
    tabs_p = _rope_tables(jnp.arange(t))
    tabs_s = _rope_tables(jnp.full((db,), n_past, I32))
    w_in_b = w_ffn_in.astype(BF16)
    w_out_b = w_ffn_out.astype(BF16)
    nsa_outs, dsa_outs = [], []
    for layer in range(depth):
        kind, j = layer % 2, layer // 2
        xp = _ffn(xp, norm_g[layer, 0], w_in_b[layer, 0], w_out_b[layer, 0])
        xs = _ffn(xs, norm_g[layer, 0], w_in_b[layer, 0], w_out_b[layer, 0])
        if kind == 0:
            xp, xs, outs = _nsa_layer(xp, xs, norm_g[layer, 1], w_nsa_in[j], w_nsa_out[j],
                                      nsa_cmp_pos[j], nsa_cmp_w1[j], nsa_cmp_w2[j], cache_nsa_cmp[j],
                                      cache_nsa_slc[j], state_nsa_win[j], page_table, tabs_p, tabs_s,
                                      b, t)
            nsa_outs.append(outs)
        else:
            xp, xs, outs = _dsa_layer(xp, xs, norm_g[layer, 1], w_dsa_in[j], w_dsa_out[j],
                                      cache_dsa_kv[j], cache_dsa_idx[j], page_table, tabs_p, tabs_s,
                                      b, t)
            dsa_outs.append(outs)
        fg = final_norm_g if layer == depth - 1 else None
        xp = _ffn(xp, norm_g[layer, 2], w_in_b[layer, 1], w_out_b[layer, 1], fg)
        xs = _ffn(xs, norm_g[layer, 2], w_in_b[layer, 1], w_out_b[layer, 1], fg)
    stack = lambda outs, k: jnp.stack([o[k] for o in outs])
    return (xp.reshape(b, t, d), xs.reshape(db, ds, d),
            stack(nsa_outs, 0), stack(nsa_outs, 1), stack(nsa_outs, 2), stack(nsa_outs, 3),
            stack(nsa_outs, 4), stack(nsa_outs, 5),
            stack(dsa_outs, 0), stack(dsa_outs, 1), stack(dsa_outs, 2), stack(dsa_outs, 3))
```

```python
import functools

import jax
import jax.numpy as jnp
from jax import lax
from jax.experimental import pallas as pl
from jax.experimental.pallas import tpu as pltpu

F32 = jnp.float32
BF16 = jnp.bfloat16
I32 = jnp.int32

D_MODEL = 1024
N_HEADS = 16
HEAD_DIM = 64
N_KV = 4
GROUP = N_HEADS // N_KV
Q_DIM = N_HEADS * HEAD_DIM
KV_DIM = N_KV * HEAD_DIM
ROT_DIM = HEAD_DIM // 4
ROPE_THETA = 500000.0
D_FF = 2816
BLOCK = 64
N_SEL = 8
WINDOW = 512
FORCE = 1.0e4
N_IDX = 8
IDX_DIM = 64
DSA_TOPK = 256
PAGE_SIZE = 128
RMS_EPS = 1e-6
NEG = -1e30

LANES = 128
KV_COLS = 2 * KV_DIM
Q_SCALE = HEAD_DIM ** -0.5
IDX_SCALE = IDX_DIM ** -0.5
INT_MIN = -(2 ** 31)
VMEM_LIMIT = 56 * 2 ** 20


def _params(sem):
    return pltpu.CompilerParams(dimension_semantics=sem, vmem_limit_bytes=VMEM_LIMIT)


def _dot(a, b):
    return jnp.dot(a, b, preferred_element_type=F32)


def _dot_nt(a, b):
    return lax.dot_general(a, b, (((1,), (1,)), ((), ())), preferred_element_type=F32)


def _iota(shape, dim):
    return lax.broadcasted_iota(I32, shape, dim)


def _rms(x, g):
    return x * lax.rsqrt(jnp.mean(x * x, axis=-1, keepdims=True) + RMS_EPS) * g


def _rope(x, c, s1, s2):
    half = ROT_DIM // 2
    outs = []
    for j in range(x.shape[1] // LANES):
        xs = x[:, j * LANES:(j + 1) * LANES]
        outs.append(xs * c + pltpu.roll(xs, LANES - half, 1) * s1 + pltpu.roll(xs, half, 1) * s2)
    return outs[0] if len(outs) == 1 else jnp.concatenate(outs, axis=1)


def _softmax_masked(s, mask):
    s = jnp.where(mask, s, NEG)
    p = jnp.where(mask, jnp.exp(s - jnp.max(s, axis=-1, keepdims=True)), 0.0)
    return p / jnp.maximum(jnp.sum(p, axis=-1, keepdims=True), 1e-30)


def _tile_rows(x, n):
    return jnp.concatenate([x] * n, axis=0)


def _ffn_body(*refs, n_ff, final):
    if final:
        x_ref, g_ref, wg_ref, wu_ref, wo_ref, fg_ref, o_ref, h_scr, acc_scr = refs
    else:
        x_ref, g_ref, wg_ref, wu_ref, wo_ref, o_ref, h_scr, acc_scr = refs
    j = pl.program_id(1)

    @pl.when(j == 0)
    def _():
        h_scr[...] = _rms(x_ref[...], g_ref[...]).astype(BF16)
        acc_scr[...] = jnp.zeros_like(acc_scr)

    h = h_scr[...]
    gate = _dot(h, wg_ref[...])
    up = _dot(h, wu_ref[...])
    act = (gate * jax.nn.sigmoid(gate) * up).astype(BF16)
    acc_scr[...] += _dot(act, wo_ref[...])

    @pl.when(j == n_ff - 1)
    def _():
        y = x_ref[...] + 0.5 * acc_scr[...]
        if final:
            y = _rms(y, fg_ref[...])
        o_ref[...] = y


def _ffn(x, g, w_in, w_out, final_g=None):
    n, d = x.shape
    tm = min(512, n)
    n_ff = 2
    fc = D_FF // n_ff
    in_specs = [
        pl.BlockSpec((tm, d), lambda i, j: (i, 0)),
        pl.BlockSpec((1, d), lambda i, j: (0, 0)),
        pl.BlockSpec((d, fc), lambda i, j: (0, j)),
        pl.BlockSpec((d, fc), lambda i, j: (0, j + n_ff)),
        pl.BlockSpec((fc, d), lambda i, j: (j, 0)),
    ]
    args = [x, g.reshape(1, d), w_in, w_in, w_out]
    if final_g is not None:
        in_specs.append(pl.BlockSpec((1, d), lambda i, j: (0, 0)))
        args.append(final_g.reshape(1, d))
    return pl.pallas_call(
        functools.partial(_ffn_body, n_ff=n_ff, final=final_g is not None),
        grid=(n // tm, n_ff),
        in_specs=in_specs,
        out_specs=pl.BlockSpec((tm, d), lambda i, j: (i, 0)),
        out_shape=jax.ShapeDtypeStruct((n, d), F32),
        scratch_shapes=[pltpu.VMEM((tm, d), BF16), pltpu.VMEM((tm, d), F32)],
        compiler_params=_params(("parallel", "arbitrary")),
        name="ffn",
    )(*args)


def _oproj_body(x_ref, o_ref, w_ref, y_ref):
    y_ref[...] = x_ref[...] + _dot(o_ref[...], w_ref[...])


def _oproj(x, o, w):
    n, d = x.shape
    tm = min(512, n)
    return pl.pallas_call(
        _oproj_body,
        grid=(n // tm,),
        in_specs=[pl.BlockSpec((tm, d), lambda i: (i, 0)),
                  pl.BlockSpec((tm, Q_DIM), lambda i: (i, 0)),
                  pl.BlockSpec((Q_DIM, d), lambda i: (0, 0))],
        out_specs=pl.BlockSpec((tm, d), lambda i: (i, 0)),
        out_shape=jax.ShapeDtypeStruct((n, d), F32),
        compiler_params=_params(("parallel",)),
        name="oproj",
    )(x, o, w)


def _nsa_proj_body(x_ref, g_ref, wq_ref, wkv_ref, wgt_ref, c_ref, s1_ref, s2_ref,
                   q_ref, qr_ref, gt_ref, cmp_ref, slc_ref, win_ref, slcb_ref, winb_ref):
    h = _rms(x_ref[...], g_ref[...]).astype(BF16)
    c, s1, s2 = c_ref[...], s1_ref[...], s2_ref[...]
    q = _dot(h, wq_ref[...])
    q_ref[...] = (q * Q_SCALE).astype(BF16)
    qr_ref[...] = (_rope(q, c, s1, s2) * Q_SCALE).astype(BF16)
    gt_ref[...] = jax.nn.sigmoid(_dot(h, wgt_ref[...]))
    kv = _dot(h, wkv_ref[...])
    cmp_ref[...] = kv[:, :KV_COLS]
    for j, (f_ref, b_ref) in enumerate(((slc_ref, slcb_ref), (win_ref, winb_ref))):
        base = (j + 1) * KV_COLS
        kr = _rope(kv[:, base:base + KV_DIM], c, s1, s2)
        row = jnp.concatenate([kr, kv[:, base + KV_DIM:base + KV_COLS]], axis=1)
        f_ref[...] = row
        b_ref[...] = row.astype(BF16)


def _nsa_proj(x, g, wq, wkv, wgt, tabs, n_tab_blocks):
    n, d = x.shape
    tm = min(512, n)
    row = lambda w: pl.BlockSpec((tm, w), lambda i: (i, 0))
    full = lambda a: pl.BlockSpec(a.shape, lambda i: (0, 0))
    tab = pl.BlockSpec((tm, LANES), lambda i: (i % n_tab_blocks, 0))
    outs = [(Q_DIM, BF16), (Q_DIM, BF16), (2 * LANES, F32), (KV_COLS, F32), (KV_COLS, F32),
            (KV_COLS, F32), (KV_COLS, BF16), (KV_COLS, BF16)]
    return pl.pallas_call(
        _nsa_proj_body,
        grid=(n // tm,),
        in_specs=[row(d), pl.BlockSpec((1, d), lambda i: (0, 0)), full(wq), full(wkv), full(wgt),
                  tab, tab, tab],
        out_specs=[row(w) for w, _ in outs],
        out_shape=[jax.ShapeDtypeStruct((n, w), t) for w, t in outs],
        compiler_params=_params(("parallel",)),
        name="nsa_proj",
    )(x, g.reshape(1, d), wq, wkv, wgt, *tabs)


def _dsa_proj_body(x_ref, g_ref, wq_ref, wkv_ref, wqi_ref, wki_ref, wwi_ref, c_ref, s1_ref, s2_ref,
                   q_ref, kv_ref, kvb_ref, qi_ref, ki_ref, kib_ref, wi_ref):
    h = _rms(x_ref[...], g_ref[...]).astype(BF16)
    c, s1, s2 = c_ref[...], s1_ref[...], s2_ref[...]
    q_ref[...] = (_rope(_dot(h, wq_ref[...]), c, s1, s2) * Q_SCALE).astype(BF16)
    kv = _dot(h, wkv_ref[...])
    row = jnp.concatenate([_rope(kv[:, :KV_DIM], c, s1, s2), kv[:, KV_DIM:]], axis=1)
    kv_ref[...] = row
    kvb_ref[...] = row.astype(BF16)
    qi_ref[...] = (_rope(_dot(h, wqi_ref[...]), c, s1, s2) * IDX_SCALE).astype(BF16)
    ki = _rope(_dot(h, wki_ref[...]), c, s1, s2)[:, :IDX_DIM]
    ki_ref[...] = ki
    kib_ref[...] = ki.astype(BF16)
    wi_ref[...] = _dot(h, wwi_ref[...]) * N_IDX ** -0.5


def _dsa_proj(x, g, wq, wkv, wqi, wki, wwi, tabs, n_tab_blocks):
    n, d = x.shape
    tm = min(512, n)
    row = lambda w: pl.BlockSpec((tm, w), lambda i: (i, 0))
    full = lambda a: pl.BlockSpec(a.shape, lambda i: (0, 0))
    tab = pl.BlockSpec((tm, LANES), lambda i: (i % n_tab_blocks, 0))
    outs = [(Q_DIM, BF16), (KV_COLS, F32), (KV_COLS, BF16), (N_IDX * LANES, BF16),
            (IDX_DIM, F32), (IDX_DIM, BF16), (LANES, F32)]
    return pl.pallas_call(
        _dsa_proj_body,
        grid=(n // tm,),
        in_specs=[row(d), pl.BlockSpec((1, d), lambda i: (0, 0)), full(wq), full(wkv), full(wqi),
                  full(wki), full(wwi), tab, tab, tab],
        out_specs=[row(w) for w, _ in outs],
        out_shape=[jax.ShapeDtypeStruct((n, w), t) for w, t in outs],
        compiler_params=_params(("parallel",)),
        name="dsa_proj",
    )(x, g.reshape(1, d), wq, wkv, wqi, wki, wwi, *tabs)


def _compress_body(*refs, n_in, nblk, paged):
    if paged:
        refs = refs[1:]
    x_refs = refs[:n_in]
    pos_ref, w1_ref, w2_ref, o_ref = refs[n_in:n_in + 4]
    x_scr = refs[n_in + 4]
    rows = x_refs[0].shape[1]
    for p in range(n_in):
        for j in range(KV_COLS // LANES):
            x_scr[j, p * rows:(p + 1) * rows, :] = x_refs[p][0, :, j * LANES:(j + 1) * LANES]
    load = lambda l, col: x_scr[col // LANES, pl.ds(l, nblk, stride=BLOCK), :]
    outs = []
    for c in range(2):
        acc = jnp.zeros((N_KV * nblk, w1_ref.shape[-1]), F32)
        for l in range(BLOCK):
            pieces = []
            for kp in range(N_KV // 2):
                col = c * KV_DIM + kp * LANES
                a = (load(l, col) + pos_ref[l:l + 1, col:col + LANES]).astype(BF16)
                pieces += [a[:, :HEAD_DIM], a[:, HEAD_DIM:]]
            acc = acc + _dot(jnp.concatenate(pieces, axis=0), w1_ref[c, l])
        out = _dot(jax.nn.gelu(acc).astype(BF16), w2_ref[c])
        outs += [out[k * nblk:(k + 1) * nblk] for k in range(N_KV)]
    o_ref[...] = jnp.concatenate(outs, axis=1).astype(BF16)


def _compress_rows(x3, pos, w1, w2):
    g, r, _ = x3.shape
    nblk = r // BLOCK
    full = lambda a: pl.BlockSpec(a.shape, lambda i: (0,) * a.ndim)
    return pl.pallas_call(
        functools.partial(_compress_body, n_in=1, nblk=nblk, paged=False),
        grid=(g,),
        in_specs=[pl.BlockSpec((1, r, KV_COLS), lambda i: (i, 0, 0)), full(pos), full(w1), full(w2)],
        out_specs=pl.BlockSpec((nblk, KV_COLS), lambda i: (i, 0)),
        out_shape=jax.ShapeDtypeStruct((g * nblk, KV_COLS), BF16),
        scratch_shapes=[pltpu.VMEM((KV_COLS // LANES, r, LANES), F32)],
        compiler_params=_params(("parallel",)),
        name="compress_rows",
    )(x3, pos, w1, w2)


def _compress_paged_body(pt_ref, *refs, n_in):
    del pt_ref
    x_refs = refs[:n_in]
    pos_ref, w1_ref, w2_ref, o_ref, x_scr = refs[n_in:]
    per_page = N_KV * HEAD_DIM
    for p in range(n_in):
        for c in range(2):
            x_scr[c, p * per_page:(p + 1) * per_page, :] = x_refs[p][0, c].reshape(per_page, PAGE_SIZE)
    m = n_in * N_KV
    for c in range(2):
        acc = jnp.zeros((2 * m, w1_ref.shape[-1]), F32)
        for d in range(HEAD_DIM):
            a = (x_scr[c, pl.ds(d, m, stride=HEAD_DIM), :] + pos_ref[c, d:d + 1, :]).astype(BF16)
            acc = acc + _dot(jnp.concatenate([a[:, :BLOCK], a[:, BLOCK:]], axis=0), w1_ref[c, d])
        o_ref[0, c] = _dot(jax.nn.gelu(acc).astype(BF16), w2_ref[c]).astype(BF16)


def _compress_paged(pool, page_table, pos, w1, w2):
    db, n_pages = page_table.shape
    per_step = 2
    n_in = per_step * n_pages
    steps = db // per_step
    rows = 2 * n_in * N_KV
    full = lambda a: pl.BlockSpec(a.shape, lambda i, pt: (0,) * a.ndim)
    page = lambda p: pl.BlockSpec(
        (1, 2, N_KV, HEAD_DIM, PAGE_SIZE),
        lambda i, pt: (pt[per_step * i + p // n_pages, p % n_pages], 0, 0, 0, 0))
    out = pl.pallas_call(
        functools.partial(_compress_paged_body, n_in=n_in),
        grid_spec=pltpu.PrefetchScalarGridSpec(
            num_scalar_prefetch=1,
            grid=(steps,),
            in_specs=[page(p) for p in range(n_in)] + [full(pos), full(w1), full(w2)],
            out_specs=pl.BlockSpec((1, 2, rows, HEAD_DIM), lambda i, pt: (i, 0, 0, 0)),
            scratch_shapes=[pltpu.VMEM((2, n_in * N_KV * HEAD_DIM, PAGE_SIZE), F32)]),
        out_shape=jax.ShapeDtypeStruct((steps, 2, rows, HEAD_DIM), BF16),
        compiler_params=_params(("parallel",)),
        name="compress_paged",
    )(page_table, *([pool] * n_in), pos, w1, w2)
    out = out.reshape(steps, 2, 2, per_step, n_pages, N_KV, HEAD_DIM)
    return out.transpose(0, 3, 4, 2, 1, 5, 6).reshape(db * n_pages * 2, KV_COLS)


def _pair_rows(blk, half, tq):
    lane = _iota((1, LANES), 1)
    keep = (lane < HEAD_DIM) if half == 0 else (lane >= HEAD_DIM)
    zero = jnp.zeros((tq, LANES), blk.dtype)
    return jnp.concatenate(
        [jnp.where(keep, blk[:, g * LANES:(g + 1) * LANES], zero) for g in range(GROUP)], axis=0)


def _select_top(score, n):
    idx = _iota(score.shape, 1).astype(F32)
    sel = jnp.zeros(score.shape, F32)
    for _ in range(n):
        m = jnp.max(score, axis=-1, keepdims=True)
        j = jnp.min(jnp.where(score == m, idx, 1e9), axis=-1, keepdims=True)
        hit = idx == j
        sel = jnp.where(hit, jnp.where(m >= 0.0, 1.0, 0.0), sel)
        score = jnp.where(hit, -2.0, score)
    return sel


def _online_attend(q_rows, get_k, get_v, get_bias, n_chunks, kt):
    r = q_rows.shape[0]

    def body(c, carry):
        m, l, acc = carry
        k, v = get_k(c), get_v(c)
        s = (_dot(q_rows, k) if kt else _dot_nt(q_rows, k)) + get_bias(c)
        m_new = jnp.maximum(m, jnp.max(s, axis=-1, keepdims=True))
        alpha = jnp.exp(m - m_new)
        p = jnp.exp(s - m_new)
        l = alpha * l + jnp.sum(p, axis=-1, keepdims=True)
        pb = p.astype(BF16)
        acc = alpha * acc + (_dot_nt(pb, v) if kt else _dot(pb, v))
        return m_new, l, acc

    init = (jnp.full((r, 1), NEG, F32), jnp.zeros((r, 1), F32), jnp.zeros((r, LANES), F32))
    if isinstance(n_chunks, int):
        carry = init
        for c in range(n_chunks):
            carry = body(c, carry)
        _, l, acc = carry
    else:
        _, l, acc = lax.fori_loop(0, n_chunks, body, init)
    return acc / jnp.maximum(l, 1e-30)


def _nsa_core(q_blk, qr_blk, gates, kc, vc, get_k, get_v, n_chunks, ck, wk, wv, wstart, qpos_t,
              kt=False):
    tq = q_blk.shape[0]
    nb = kc.shape[0]
    wlen = wk.shape[1] if kt else wk.shape[0]
    lane = _iota((1, LANES), 1)
    qpos_r = _tile_rows(qpos_t, GROUP)
    blk = _iota((1, nb), 1)
    cur = jnp.right_shift(qpos_t, 6)
    per_half = []
    for half in range(2):
        q_rows = _pair_rows(q_blk, half, tq)
        qr_rows = _pair_rows(qr_blk, half, tq)
        cmask = ((blk + 1) * BLOCK - 1) <= qpos_r
        p_cmp = _softmax_masked(_dot_nt(q_rows, kc), cmask)
        o_cmp = _dot(p_cmp.astype(BF16), vc)
        imp = p_cmp[0:tq]
        for g in range(1, GROUP):
            imp = imp + p_cmp[g * tq:(g + 1) * tq]
        forced = (blk == 0) | (blk == cur) | (blk == cur - 1)
        score = jnp.where(forced, FORCE, imp)
        score = jnp.where(blk <= cur, score, -1.0)
        sel = _select_top(score, min(N_SEL, nb)).astype(BF16)

        def slc_bias(c, sel=sel):
            kpos = c * ck + _iota((1, ck), 1)
            expand = jnp.where(jnp.right_shift(kpos, 6) == _iota((nb, 1), 0), 1.0, 0.0).astype(BF16)
            chosen = _dot(sel, expand)
            bias = jnp.where((chosen > 0.5) & (kpos <= qpos_t), 0.0, NEG)
            return _tile_rows(bias, GROUP)

        o_slc = _online_attend(qr_rows, get_k, get_v, slc_bias, n_chunks, kt)
        kwpos = wstart + _iota((1, wlen), 1)
        dpos = qpos_r - kwpos
        wmask = (dpos >= 0) & (dpos <= WINDOW) & (kwpos >= 0)
        p_win = _softmax_masked(_dot(qr_rows, wk) if kt else _dot_nt(qr_rows, wk), wmask).astype(BF16)
        o_win = _dot_nt(p_win, wv) if kt else _dot(p_win, wv)
        outs = []
        for g in range(GROUP):
            c0 = half * GROUP * 3 + g * 3
            sl = slice(g * tq, (g + 1) * tq)
            outs.append(gates[:, c0:c0 + 1] * o_cmp[sl] + gates[:, c0 + 1:c0 + 2] * o_slc[sl]
                        + gates[:, c0 + 2:c0 + 3] * o_win[sl])
        per_half.append(outs)
    return jnp.concatenate(
        [jnp.where(lane < HEAD_DIM, per_half[0][g], per_half[1][g]) for g in range(GROUP)], axis=1)


def _nsa_prompt_body(q_ref, qr_ref, gt_ref, kc_ref, vc_ref, sk_ref, sv_ref, wk_ref, wv_ref, o_ref,
                     *, tq, ck):
    s0 = pl.program_id(2) * tq
    qpos_t = s0 + _iota((tq, 1), 0)
    n_chunks = (s0 + tq + ck - 1) // ck
    wstart = pl.multiple_of(jnp.maximum(s0 - WINDOW, 0), tq)
    chunk = lambda ref: (lambda c: ref[pl.ds(pl.multiple_of(c * ck, ck), ck), :])
    out = _nsa_core(q_ref[...], qr_ref[...], gt_ref[...], kc_ref[...], vc_ref[...],
                    chunk(sk_ref), chunk(sv_ref), n_chunks, ck,
                    wk_ref[pl.ds(wstart, WINDOW + tq), :], wv_ref[pl.ds(wstart, WINDOW + tq), :],
                    wstart, qpos_t)
    o_ref[...] = out.astype(BF16)


def _nsa_attn_prompt(q, qr, gates, kvc, slc_b, win_b, b, t):
    tq = 128
    ck = min(512, t)
    nt = t // tq
    nb = t // BLOCK
    qspec = pl.BlockSpec((tq, 4 * LANES), lambda bi, kp, i: (bi * nt + i, kp))
    seq = lambda off: pl.BlockSpec((t, LANES), lambda bi, kp, i: (bi, kp + off))
    return pl.pallas_call(
        functools.partial(_nsa_prompt_body, tq=tq, ck=ck),
        grid=(b, 2, nt),
        in_specs=[qspec, qspec,
                  pl.BlockSpec((tq, LANES), lambda bi, kp, i: (bi * nt + i, kp)),
                  pl.BlockSpec((nb, LANES), lambda bi, kp, i: (bi, kp)),
                  pl.BlockSpec((nb, LANES), lambda bi, kp, i: (bi, kp + 2)),
                  seq(0), seq(2), seq(0), seq(2)],
        out_specs=qspec,
        out_shape=jax.ShapeDtypeStruct((b * t, Q_DIM), BF16),
        compiler_params=_params(("parallel", "parallel", "arbitrary")),
        name="nsa_attn_prompt",
    )(q, qr, gates, kvc, kvc, slc_b, slc_b, win_b, win_b)


def _tail_rows(row, n, dtype):
    w = row.shape[-1]
    first = _iota((n, w), 0) == 0
    return jnp.where(first, jnp.broadcast_to(row, (n, w)), 0.0).astype(dtype)


def _tail_cols(row, n):
    return _tail_rows(row, n, F32).T.reshape(2, N_KV, HEAD_DIM, n).astype(BF16)


def _feature_major(a):
    return jnp.moveaxis(a, -4, -1)


def _nsa_sample_body(pt_ref, q_ref, qr_ref, gt_ref, kcp_ref, kcn_ref, *rest,
                     n_pages, n_past, wb, tq, tail):
    del pt_ref
    pages = rest[:n_pages]
    slcn_ref, wst_ref, winn_ref, o_ref, kc_scr, slc_scr, win_scr = rest[n_pages:]
    nbp = kcp_ref.shape[0]
    kc_scr[...] = jnp.zeros_like(kc_scr)
    kc_scr[0:nbp, :] = kcp_ref[...]
    kc_scr[nbp:nbp + 16, :] = _tail_rows(kcn_ref[0].astype(F32), 16, BF16)
    for p in range(n_pages):
        slc_scr[:, :, :, p * PAGE_SIZE:(p + 1) * PAGE_SIZE] = pages[p][0].astype(BF16)
    slc_scr[:, :, :, n_past:n_past + tail] = _tail_cols(slcn_ref[0], tail)
    win_scr[:, :, :, 0:wb] = wst_ref[0].astype(BF16)
    win_scr[:, :, :, wb:wb + tail] = _tail_cols(winn_ref[0], tail)
    qpos_t = jnp.full((tq, 1), n_past, I32)
    ck = n_past + tail
    pair = lambda scr, c, kp: scr[c, 2 * kp:2 * kp + 2].reshape(2 * HEAD_DIM, scr.shape[-1])
    outs = []
    for kp in range(2):
        cols = slice(kp * 4 * LANES, (kp + 1) * 4 * LANES)
        kcol = slice(kp * LANES, (kp + 1) * LANES)
        vcol = slice((kp + 2) * LANES, (kp + 3) * LANES)
        outs.append(_nsa_core(
            jnp.broadcast_to(q_ref[0][:, cols], (tq, 4 * LANES)),
            jnp.broadcast_to(qr_ref[0][:, cols], (tq, 4 * LANES)),
            jnp.broadcast_to(gt_ref[0][:, kcol], (tq, LANES)),
            kc_scr[:, kcol], kc_scr[:, vcol],
            lambda c, kp=kp: pair(slc_scr, 0, kp), lambda c, kp=kp: pair(slc_scr, 1, kp), 1, ck,
            pair(win_scr, 0, kp), pair(win_scr, 1, kp), n_past - wb, qpos_t, kt=True))
    o_ref[0] = jnp.concatenate(outs, axis=1)[0:1].astype(BF16)


def _nsa_attn_sample(q, qr, gates, kvc_past, kvc_new, cache_slc, page_table, slc_new, win_state,
                     win_new):
    db, n_pages = page_table.shape
    n_past = n_pages * PAGE_SIZE
    wb = win_state.shape[-1]
    nbp = n_past // BLOCK
    tq, tail = 8, PAGE_SIZE
    nb = 64
    assert nb * BLOCK >= n_past + tail and nbp + 16 <= nb
    one = lambda w: pl.BlockSpec((1, 1, w), lambda i, pt: (i, 0, 0))
    fm = lambda n: (1, 2, N_KV, HEAD_DIM, n)
    page = lambda p: pl.BlockSpec(fm(PAGE_SIZE), lambda i, pt: (pt[i, p], 0, 0, 0, 0))
    r3 = lambda a: a.reshape(db, 1, a.shape[-1])
    return pl.pallas_call(
        functools.partial(_nsa_sample_body, n_pages=n_pages, n_past=n_past, wb=wb, tq=tq, tail=tail),
        grid_spec=pltpu.PrefetchScalarGridSpec(
            num_scalar_prefetch=1,
            grid=(db,),
            in_specs=[one(Q_DIM), one(Q_DIM), one(2 * LANES),
                      pl.BlockSpec((nbp, KV_COLS), lambda i, pt: (i, 0)), one(KV_COLS)]
                     + [page(p) for p in range(n_pages)]
                     + [one(KV_COLS), pl.BlockSpec(fm(wb), lambda i, pt: (i, 0, 0, 0, 0)),
                        one(KV_COLS)],
            out_specs=one(Q_DIM),
            scratch_shapes=[pltpu.VMEM((nb, KV_COLS), BF16),
                            pltpu.VMEM(fm(n_past + tail)[1:], BF16),
                            pltpu.VMEM(fm(wb + tail)[1:], BF16)]),
        out_shape=jax.ShapeDtypeStruct((db, 1, Q_DIM), BF16),
        compiler_params=_params(("parallel",)),
        name="nsa_attn_sample",
    )(page_table, r3(q), r3(qr), r3(gates), kvc_past, r3(kvc_new), *([cache_slc] * n_pages),
      r3(slc_new), win_state, r3(win_new)).reshape(db, Q_DIM)


def _dsa_core(q_blk, qi_blk, wi, get_ki, get_k, get_v, n_chunks, ck, qpos_t, key_scr, topk, kt=False):
    tq = q_blk.shape[0]
    qi_rows = jnp.concatenate(
        [qi_blk[:, h * LANES:h * LANES + IDX_DIM] for h in range(N_IDX)], axis=0)
    wcol = [wi[:, h:h + 1] for h in range(N_IDX)]

    def loop(fn, init):
        if isinstance(n_chunks, int):
            for c in range(n_chunks):
                init = fn(c, init)
            return init
        return lax.fori_loop(0, n_chunks, fn, init)

    def kpos_of(c):
        return c * ck + _iota((1, ck), 1)

    def score_chunk(c, carry):
        sc = _dot(qi_rows, get_ki(c)) if kt else _dot_nt(qi_rows, get_ki(c))
        score = jnp.maximum(sc[0:tq], 0.0) * wcol[0]
        for h in range(1, N_IDX):
            score = score + jnp.maximum(sc[h * tq:(h + 1) * tq], 0.0) * wcol[h]
        score = jnp.where(kpos_of(c) <= qpos_t, score, NEG)
        score = jnp.where(score == 0.0, 0.0, score)
        bits = lax.bitcast_convert_type(score, I32)
        key_scr[c] = jnp.where(bits < 0, bits ^ 0x7FFFFFFF, bits)
        return carry

    loop(score_chunk, 0)

    def count(pred):
        def fn(c, acc):
            hit = jnp.where(pred(key_scr[c], kpos_of(c)), 1.0, 0.0)
            part = hit[:, 0:LANES]
            for j in range(1, ck // LANES):
                part = part + hit[:, j * LANES:(j + 1) * LANES]
            return acc + part
        return jnp.sum(loop(fn, jnp.zeros((tq, LANES), F32)), axis=-1, keepdims=True)

    kf = float(topk)
    lo = jnp.where(count(lambda k, p: k >= 0) >= kf, 0, INT_MIN).astype(I32)

    def bit_step(it, lo):
        cand = lo + jnp.left_shift(jnp.int32(1), 30 - it)
        return jnp.where(count(lambda k, p: k >= cand) >= kf, cand, lo)

    thr = lax.fori_loop(0, 31, bit_step, lo)
    need = kf - count(lambda k, p: k > thr)
    surplus = jnp.max(count(lambda k, p: k >= thr)) > kf

    def idx_step(it, lo):
        mid = lo + jnp.left_shift(jnp.int32(1), 13 - it)
        enough = count(lambda k, p: (k == thr) & (p <= mid)) >= need
        return jnp.where(enough, lo, mid)

    jmax = lax.cond(
        surplus,
        lambda: lax.fori_loop(0, 14, idx_step, jnp.full((tq, 1), -1, I32)) + 1,
        lambda: jnp.full((tq, 1), 2 ** 14, I32))

    def bias_chunk(c, carry):
        k = key_scr[c]
        p = kpos_of(c)
        keep = ((k > thr) | ((k == thr) & (p <= jmax))) & (p <= qpos_t)
        key_scr[c] = lax.bitcast_convert_type(jnp.where(keep, 0.0, NEG), I32)
        return carry

    loop(bias_chunk, 0)

    def get_bias(c):
        return _tile_rows(lax.bitcast_convert_type(key_scr[c], F32), GROUP)

    lane = _iota((1, LANES), 1)
    chunks = []
    for kp in range(2):
        per_half = []
        for half in range(2):
            q_rows = _pair_rows(q_blk[:, kp * 4 * LANES:(kp + 1) * 4 * LANES], half, tq)
            o = _online_attend(q_rows, lambda c, kp=kp: get_k(kp, c), lambda c, kp=kp: get_v(kp, c),
                               get_bias, n_chunks, kt)
            per_half.append(o)
        for g in range(GROUP):
            sl = slice(g * tq, (g + 1) * tq)
            chunks.append(jnp.where(lane < HEAD_DIM, per_half[0][sl], per_half[1][sl]))
    return jnp.concatenate(chunks, axis=1)


def _dsa_prompt_body(q_ref, qi_ref, wi_ref, ki_ref, k0_ref, k1_ref, v0_ref, v1_ref, o_ref, key_scr,
                     *, tq, ck, topk):
    s0 = pl.program_id(1) * tq
    qpos_t = s0 + _iota((tq, 1), 0)
    n_chunks = (s0 + tq + ck - 1) // ck
    rows = lambda c: pl.ds(pl.multiple_of(c * ck, ck), ck)
    k_refs, v_refs = (k0_ref, k1_ref), (v0_ref, v1_ref)
    out = _dsa_core(q_ref[...], qi_ref[...], wi_ref[...], lambda c: ki_ref[rows(c), :],
                    lambda kp, c: k_refs[kp][rows(c), :], lambda kp, c: v_refs[kp][rows(c), :],
                    n_chunks, ck, qpos_t, key_scr, topk)
    o_ref[...] = out.astype(BF16)


def _dsa_attn_prompt(q, qi, wi, ki_b, kv_b, b, t):
    tq = 128
    ck = min(512, t)
    nt = t // tq
    row = lambda w: pl.BlockSpec((tq, w), lambda bi, i: (bi * nt + i, 0))
    seq = lambda j: pl.BlockSpec((t, LANES), lambda bi, i: (bi, j))
    return pl.pallas_call(
        functools.partial(_dsa_prompt_body, tq=tq, ck=ck, topk=min(DSA_TOPK, t // 4)),
        grid=(b, nt),
        in_specs=[row(Q_DIM), row(N_IDX * LANES), row(LANES),
                  pl.BlockSpec((t, IDX_DIM), lambda bi, i: (bi, 0)),
                  seq(0), seq(1), seq(2), seq(3)],
        out_specs=row(Q_DIM),
        out_shape=jax.ShapeDtypeStruct((b * t, Q_DIM), BF16),
        scratch_shapes=[pltpu.VMEM((t // ck, tq, ck), I32)],
        compiler_params=_params(("parallel", "arbitrary")),
        name="dsa_attn_prompt",
    )(q, qi, wi, ki_b, kv_b, kv_b, kv_b, kv_b)


def _dsa_sample_body(pt_ref, q_ref, qi_ref, wi_ref, *rest, n_pages, n_past, tq, tail, topk):
    del pt_ref
    kv_pages = rest[:n_pages]
    ki_pages = rest[n_pages:2 * n_pages]
    kvn_ref, kin_ref, o_ref, kv_scr, ki_scr, key_scr = rest[2 * n_pages:]
    for p in range(n_pages):
        cols = slice(p * PAGE_SIZE, (p + 1) * PAGE_SIZE)
        kv_scr[:, :, :, cols] = kv_pages[p][0].astype(BF16)
        ki_scr[:, cols] = ki_pages[p][0].astype(BF16)
    kv_scr[:, :, :, n_past:n_past + tail] = _tail_cols(kvn_ref[0], tail)
    ki_scr[:, n_past:n_past + tail] = _tail_rows(kin_ref[0], tail, F32).T[0:IDX_DIM].astype(BF16)
    qpos_t = jnp.full((tq, 1), n_past, I32)
    ck = n_past + tail
    pair = lambda c, kp: kv_scr[c, 2 * kp:2 * kp + 2].reshape(2 * HEAD_DIM, ck)
    out = _dsa_core(jnp.broadcast_to(q_ref[0], (tq, Q_DIM)),
                    jnp.broadcast_to(qi_ref[0], (tq, N_IDX * LANES)),
                    jnp.broadcast_to(wi_ref[0], (tq, LANES)),
                    lambda c: ki_scr[...],
                    lambda kp, c: pair(0, kp), lambda kp, c: pair(1, kp),
                    1, ck, qpos_t, key_scr, topk, kt=True)
    o_ref[0] = out[0:1].astype(BF16)


def _dsa_attn_sample(q, qi, wi, cache_kv, cache_idx, page_table, kv_new, ki_new):
    db, n_pages = page_table.shape
    n_past = n_pages * PAGE_SIZE
    tq, tail = 8, PAGE_SIZE
    one = lambda w: pl.BlockSpec((1, 1, w), lambda i, pt: (i, 0, 0))
    kv_page = lambda p: pl.BlockSpec((1, 2, N_KV, HEAD_DIM, PAGE_SIZE),
                                     lambda i, pt: (pt[i, p], 0, 0, 0, 0))
    ki_page = lambda p: pl.BlockSpec((1, IDX_DIM, PAGE_SIZE), lambda i, pt: (pt[i, p], 0, 0))
    r3 = lambda a: a.reshape(db, 1, a.shape[-1])
    return pl.pallas_call(
        functools.partial(_dsa_sample_body, n_pages=n_pages, n_past=n_past, tq=tq, tail=tail,
                          topk=min(DSA_TOPK, (n_past + 1) // 4)),
        grid_spec=pltpu.PrefetchScalarGridSpec(
            num_scalar_prefetch=1,
            grid=(db,),
            in_specs=[one(Q_DIM), one(N_IDX * LANES), one(LANES)]
                     + [kv_page(p) for p in range(n_pages)]
                     + [ki_page(p) for p in range(n_pages)]
                     + [one(KV_COLS), one(LANES)],
            out_specs=one(Q_DIM),
            scratch_shapes=[pltpu.VMEM((2, N_KV, HEAD_DIM, n_past + tail), BF16),
                            pltpu.VMEM((IDX_DIM, n_past + tail), BF16),
                            pltpu.VMEM((1, tq, n_past + tail), I32)]),
        out_shape=jax.ShapeDtypeStruct((db, 1, Q_DIM), BF16),
        compiler_params=_params(("parallel",)),
        name="dsa_attn_sample",
    )(page_table, r3(q), r3(qi), r3(wi), *([cache_kv] * n_pages), *([cache_idx] * n_pages),
      r3(kv_new), r3(ki_new)).reshape(db, Q_DIM)


def _rope_tables(pos):
    half = ROT_DIM // 2
    n = pos.shape[0]
    inv = ROPE_THETA ** (-jnp.arange(half, dtype=F32) * 2.0 / ROT_DIM)
    ang = pos.astype(F32)[:, None] * inv[None, :]
    cos, sin = jnp.cos(ang), jnp.sin(ang)
    z8 = jnp.zeros((n, half), F32)
    rest = HEAD_DIM - ROT_DIM
    c = jnp.concatenate([cos, cos, jnp.ones((n, rest), F32)], axis=1)
    s1 = jnp.concatenate([-sin, z8, jnp.zeros((n, rest), F32)], axis=1)
    s2 = jnp.concatenate([z8, sin, jnp.zeros((n, rest), F32)], axis=1)
    rep = LANES // HEAD_DIM
    return tuple(jnp.tile(a, (1, rep)) for a in (c, s1, s2))


def _pair_perm():
    cols = []
    for kp in range(N_KV // 2):
        for g in range(GROUP):
            for half in range(2):
                h = (2 * kp + half) * GROUP + g
                cols.extend(range(h * HEAD_DIM, (h + 1) * HEAD_DIM))
    return jnp.array(cols, dtype=I32)


def _pad_cols(w, n):
    return jnp.pad(w, ((0, 0), (0, n - w.shape[1])))


def _nsa_layer(xp, xs, g, w_in, w_out, cmp_pos, cmp_w1, cmp_w2, cache_cmp, cache_slc, win_state,
               page_table, tabs_p, tabs_s, b, t):
    db = xs.shape[0]
    perm = _pair_perm()
    wq = w_in[:, :Q_DIM][:, perm].astype(BF16)
    wkv = w_in[:, Q_DIM:Q_DIM + 3 * KV_COLS].astype(BF16)
    wg = w_in[:, Q_DIM + 3 * KV_COLS:]
    wgt = jnp.concatenate([_pad_cols(wg[:, kp * 2 * GROUP * 3:(kp + 1) * 2 * GROUP * 3], LANES)
                           for kp in range(N_KV // 2)], axis=1).astype(BF16)
    wo = w_out[perm, :].astype(BF16)
    pos = jnp.concatenate([jnp.tile(cmp_pos[:, c, :], (1, N_KV)) for c in range(2)], axis=1)
    w1 = cmp_w1.astype(BF16)
    w2 = cmp_w2.astype(BF16)

    q, qr, gates, kcmp, kslc, kwin, slc_b, win_b = _nsa_proj(xp, g, wq, wkv, wgt, tabs_p, t // min(512, t))
    kvc = _compress_rows(kcmp.reshape(b, t, KV_COLS), pos, w1, w2)
    o = _nsa_attn_prompt(q, qr, gates, kvc, slc_b, win_b, b, t)
    xp = _oproj(xp, o, wo)
    qs, qrs, gs, scmp, sslc, swin, _, _ = _nsa_proj(xs, g, wq, wkv, wgt, tabs_s, 1)
    pos_fm = jnp.tile(cmp_pos.transpose(1, 2, 0), (1, 1, PAGE_SIZE // BLOCK))
    w1_fm = cmp_w1.transpose(0, 2, 1, 3).astype(BF16)
    kvc_past = _compress_paged(_feature_major(cache_cmp), page_table, pos_fm, w1_fm, w2)
    new_rows = jnp.pad(scmp[:, None, :], ((0, 0), (0, BLOCK - 1), (0, 0)))
    kvc_new = _compress_rows(new_rows.reshape(2, db // 2 * BLOCK, KV_COLS), pos, w1, w2)
    os_ = _nsa_attn_sample(qs, qrs, gs, kvc_past, kvc_new, _feature_major(cache_slc), page_table,
                           sslc, _feature_major(win_state), swin)
    xs = _oproj(xs, os_, wo)
    wb = min(WINDOW, t)
    kv5 = lambda a, n: a.reshape(n, -1, 2, N_KV, HEAD_DIM)
    win_s = jnp.concatenate([win_state, kv5(swin, db)], axis=1)[:, 1:]
    outs = (kv5(kcmp, b), kv5(scmp, db), kv5(kslc, b), kv5(sslc, db),
            kv5(kwin, b)[:, t - wb:], win_s)
    return xp, xs, outs


def _dsa_layer(xp, xs, g, w_in, w_out, cache_kv, cache_idx, page_table, tabs_p, tabs_s, b, t):
    db = xs.shape[0]
    perm = _pair_perm()
    wq = w_in[:, :Q_DIM][:, perm].astype(BF16)
    o1 = Q_DIM + KV_COLS
    wkv = w_in[:, Q_DIM:o1].astype(BF16)
    o2 = o1 + N_IDX * IDX_DIM
    wqi = jnp.concatenate([_pad_cols(w_in[:, o1 + h * IDX_DIM:o1 + (h + 1) * IDX_DIM], LANES)
                           for h in range(N_IDX)], axis=1).astype(BF16)
    wki = _pad_cols(w_in[:, o2:o2 + IDX_DIM], LANES).astype(BF16)
    wwi = _pad_cols(w_in[:, o2 + IDX_DIM:], LANES).astype(BF16)
    wo = w_out[perm, :].astype(BF16)

    q, kv, kv_b, qi, ki, ki_b, wi = _dsa_proj(xp, g, wq, wkv, wqi, wki, wwi, tabs_p, t // min(512, t))
    o = _dsa_attn_prompt(q, qi, wi, ki_b, kv_b, b, t)
    xp = _oproj(xp, o, wo)
    qs, kvs, _, qis, kis, _, wis = _dsa_proj(xs, g, wq, wkv, wqi, wki, wwi, tabs_s, 1)
    os_ = _dsa_attn_sample(qs, qis, wis, _feature_major(cache_kv), jnp.swapaxes(cache_idx, -1, -2),
                           page_table, kvs, _pad_cols(kis, LANES))
    xs = _oproj(xs, os_, wo)
    outs = (kv.reshape(b, t, 2, N_KV, HEAD_DIM), kvs.reshape(db, 1, 2, N_KV, HEAD_DIM),
            ki.reshape(b, t, IDX_DIM), kis.reshape(db, 1, IDX_DIM))
    return xp, xs, outs


def kernel(x_prompt, x_sample, cache_nsa_cmp, cache_nsa_slc, state_nsa_win, cache_dsa_kv, cache_dsa_idx,
           page_table, norm_g, w_ffn_in, w_ffn_out, w_nsa_in, w_nsa_out, nsa_cmp_pos, nsa_cmp_w1,
           nsa_cmp_w2, w_dsa_in, w_dsa_out, final_norm_g):
    b, t, d = x_prompt.shape
    db, ds, _ = x_sample.shape
    assert ds == 1
    depth = norm_g.shape[0]
    n_past = page_table.shape[1] * PAGE_SIZE
    xp = x_prompt.reshape(b * t, d)
    xs = x_sample.reshape(db, d)
    tabs_p = _rope_tables(jnp.arange(t))
    tabs_s = _rope_tables(jnp.full((db,), n_past, I32))
    w_in_b = w_ffn_in.astype(BF16)
    w_out_b = w_ffn_out.astype(BF16)
    nsa_outs, dsa_outs = [], []
    for layer in range(depth):
        kind, j = layer % 2, layer // 2
        xp = _ffn(xp, norm_g[layer, 0], w_in_b[layer, 0], w_out_b[layer, 0])
        xs = _ffn(xs, norm_g[layer, 0], w_in_b[layer, 0], w_out_b[layer, 0])
        if kind == 0:
            xp, xs, outs = _nsa_layer(xp, xs, norm_g[layer, 1], w_nsa_in[j], w_nsa_out[j],
                                      nsa_cmp_pos[j], nsa_cmp_w1[j], nsa_cmp_w2[j], cache_nsa_cmp[j],
                                      cache_nsa_slc[j], state_nsa_win[j], page_table, tabs_p, tabs_s,
                                      b, t)
            nsa_outs.append(outs)
        else:
            xp, xs, outs = _dsa_layer(xp, xs, norm_g[layer, 1], w_dsa_in[j], w_dsa_out[j],
                                      cache_dsa_kv[j], cache_dsa_idx[j], page_table, tabs_p, tabs_s,
                                      b, t)
            dsa_outs.append(outs)
        fg = final_norm_g if layer == depth - 1 else None
        xp = _ffn(xp, norm_g[layer, 2], w_in_b[layer, 1], w_out_b[layer, 1], fg)
        xs = _ffn(xs, norm_g[layer, 2], w_in_b[layer, 1], w_out_b[layer, 1], fg)
    stack = lambda outs, k: jnp.stack([o[k] for o in outs])
    return (xp.reshape(b, t, d), xs.reshape(db, ds, d),
            stack(nsa_outs, 0), stack(nsa_outs, 1), stack(nsa_outs, 2), stack(nsa_outs, 3),
            stack(nsa_outs, 4), stack(nsa_outs, 5),
            stack(dsa_outs, 0), stack(dsa_outs, 1), stack(dsa_outs, 2), stack(dsa_outs, 3))
```

```python
import functools

import jax
import jax.numpy as jnp
from jax import lax
from jax.experimental import pallas as pl
from jax.experimental.pallas import tpu as pltpu

F32 = jnp.float32
BF16 = jnp.bfloat16
I32 = jnp.int32

D_MODEL = 1024
N_HEADS = 16
HEAD_DIM = 64
N_KV = 4
GROUP = N_HEADS // N_KV
Q_DIM = N_HEADS * HEAD_DIM
KV_DIM = N_KV * HEAD_DIM
ROT_DIM = HEAD_DIM // 4
ROPE_THETA = 500000.0
D_FF = 2816
BLOCK = 64
N_SEL = 8
WINDOW = 512
FORCE = 1.0e4
N_IDX = 8
IDX_DIM = 64
DSA_TOPK = 256
PAGE_SIZE = 128
RMS_EPS = 1e-6
NEG = -1e30

LANES = 128
KV_COLS = 2 * KV_DIM
Q_SCALE = HEAD_DIM ** -0.5
IDX_SCALE = IDX_DIM ** -0.5
INT_MIN = -(2 ** 31)
VMEM_LIMIT = 56 * 2 ** 20


def _params(sem):
    return pltpu.CompilerParams(dimension_semantics=sem, vmem_limit_bytes=VMEM_LIMIT)


def _dot(a, b):
    return jnp.dot(a, b, preferred_element_type=F32)


def _dot_nt(a, b):
    return lax.dot_general(a, b, (((1,), (1,)), ((), ())), preferred_element_type=F32)


def _iota(shape, dim):
    return lax.broadcasted_iota(I32, shape, dim)


def _rms(x, g):
    return x * lax.rsqrt(jnp.mean(x * x, axis=-1, keepdims=True) + RMS_EPS) * g


def _rope(x, c, s1, s2):
    half = ROT_DIM // 2
    outs = []
    for j in range(x.shape[1] // LANES):
        xs = x[:, j * LANES:(j + 1) * LANES]
        outs.append(xs * c + pltpu.roll(xs, LANES - half, 1) * s1 + pltpu.roll(xs, half, 1) * s2)
    return outs[0] if len(outs) == 1 else jnp.concatenate(outs, axis=1)


def _softmax_masked(s, mask):
    s = jnp.where(mask, s, NEG)
    p = jnp.where(mask, jnp.exp(s - jnp.max(s, axis=-1, keepdims=True)), 0.0)
    return p / jnp.maximum(jnp.sum(p, axis=-1, keepdims=True), 1e-30)


def _tile_rows(x, n):
    return jnp.concatenate([x] * n, axis=0)


def _ffn_body(*refs, n_ff, final):
    if final:
        x_ref, g_ref, wg_ref, wu_ref, wo_ref, fg_ref, o_ref, h_scr, acc_scr = refs
    else:
        x_ref, g_ref, wg_ref, wu_ref, wo_ref, o_ref, h_scr, acc_scr = refs
    j = pl.program_id(1)

    @pl.when(j == 0)
    def _():
        h_scr[...] = _rms(x_ref[...], g_ref[...]).astype(BF16)
        acc_scr[...] = jnp.zeros_like(acc_scr)

    h = h_scr[...]
    gate = _dot(h, wg_ref[...])
    up = _dot(h, wu_ref[...])
    act = (gate * jax.nn.sigmoid(gate) * up).astype(BF16)
    acc_scr[...] += _dot(act, wo_ref[...])

    @pl.when(j == n_ff - 1)
    def _():
        y = x_ref[...] + 0.5 * acc_scr[...]
        if final:
            y = _rms(y, fg_ref[...])
        o_ref[...] = y


def _ffn(x, g, w_in, w_out, final_g=None):
    n, d = x.shape
    tm = min(512, n)
    n_ff = 2
    fc = D_FF // n_ff
    in_specs = [
        pl.BlockSpec((tm, d), lambda i, j: (i, 0)),
        pl.BlockSpec((1, d), lambda i, j: (0, 0)),
        pl.BlockSpec((d, fc), lambda i, j: (0, j)),
        pl.BlockSpec((d, fc), lambda i, j: (0, j + n_ff)),
        pl.BlockSpec((fc, d), lambda i, j: (j, 0)),
    ]
    args = [x, g.reshape(1, d), w_in, w_in, w_out]
    if final_g is not None:
        in_specs.append(pl.BlockSpec((1, d), lambda i, j: (0, 0)))
        args.append(final_g.reshape(1, d))
    return pl.pallas_call(
        functools.partial(_ffn_body, n_ff=n_ff, final=final_g is not None),
        grid=(n // tm, n_ff),
        in_specs=in_specs,
        out_specs=pl.BlockSpec((tm, d), lambda i, j: (i, 0)),
        out_shape=jax.ShapeDtypeStruct((n, d), F32),
        scratch_shapes=[pltpu.VMEM((tm, d), BF16), pltpu.VMEM((tm, d), F32)],
        compiler_params=_params(("parallel", "arbitrary")),
        name="ffn",
    )(*args)


def _oproj_body(x_ref, o_ref, w_ref, y_ref):
    y_ref[...] = x_ref[...] + _dot(o_ref[...], w_ref[...])


def _oproj(x, o, w):
    n, d = x.shape
    tm = min(512, n)
    return pl.pallas_call(
        _oproj_body,
        grid=(n // tm,),
        in_specs=[pl.BlockSpec((tm, d), lambda i: (i, 0)),
                  pl.BlockSpec((tm, Q_DIM), lambda i: (i, 0)),
                  pl.BlockSpec((Q_DIM, d), lambda i: (0, 0))],
        out_specs=pl.BlockSpec((tm, d), lambda i: (i, 0)),
        out_shape=jax.ShapeDtypeStruct((n, d), F32),
        compiler_params=_params(("parallel",)),
        name="oproj",
    )(x, o, w)


def _nsa_proj_body(x_ref, g_ref, wq_ref, wkv_ref, wgt_ref, c_ref, s1_ref, s2_ref,
                   q_ref, qr_ref, gt_ref, cmp_ref, slc_ref, win_ref, slcb_ref, winb_ref):
    h = _rms(x_ref[...], g_ref[...]).astype(BF16)
    c, s1, s2 = c_ref[...], s1_ref[...], s2_ref[...]
    q = _dot(h, wq_ref[...])
    q_ref[...] = (q * Q_SCALE).astype(BF16)
    qr_ref[...] = (_rope(q, c, s1, s2) * Q_SCALE).astype(BF16)
    gt_ref[...] = jax.nn.sigmoid(_dot(h, wgt_ref[...]))
    kv = _dot(h, wkv_ref[...])
    cmp_ref[...] = kv[:, :KV_COLS]
    for j, (f_ref, b_ref) in enumerate(((slc_ref, slcb_ref), (win_ref, winb_ref))):
        base = (j + 1) * KV_COLS
        kr = _rope(kv[:, base:base + KV_DIM], c, s1, s2)
        row = jnp.concatenate([kr, kv[:, base + KV_DIM:base + KV_COLS]], axis=1)
        f_ref[...] = row
        b_ref[...] = row.astype(BF16)


def _nsa_proj(x, g, wq, wkv, wgt, tabs, n_tab_blocks):
    n, d = x.shape
    tm = min(512, n)
    row = lambda w: pl.BlockSpec((tm, w), lambda i: (i, 0))
    full = lambda a: pl.BlockSpec(a.shape, lambda i: (0, 0))
    tab = pl.BlockSpec((tm, LANES), lambda i: (i % n_tab_blocks, 0))
    outs = [(Q_DIM, BF16), (Q_DIM, BF16), (2 * LANES, F32), (KV_COLS, F32), (KV_COLS, F32),
            (KV_COLS, F32), (KV_COLS, BF16), (KV_COLS, BF16)]
    return pl.pallas_call(
        _nsa_proj_body,
        grid=(n // tm,),
        in_specs=[row(d), pl.BlockSpec((1, d), lambda i: (0, 0)), full(wq), full(wkv), full(wgt),
                  tab, tab, tab],
        out_specs=[row(w) for w, _ in outs],
        out_shape=[jax.ShapeDtypeStruct((n, w), t) for w, t in outs],
        compiler_params=_params(("parallel",)),
        name="nsa_proj",
    )(x, g.reshape(1, d), wq, wkv, wgt, *tabs)


def _dsa_proj_body(x_ref, g_ref, wq_ref, wkv_ref, wqi_ref, wki_ref, wwi_ref, c_ref, s1_ref, s2_ref,
                   q_ref, kv_ref, kvb_ref, qi_ref, ki_ref, kib_ref, wi_ref):
    h = _rms(x_ref[...], g_ref[...]).astype(BF16)
    c, s1, s2 = c_ref[...], s1_ref[...], s2_ref[...]
    q_ref[...] = (_rope(_dot(h, wq_ref[...]), c, s1, s2) * Q_SCALE).astype(BF16)
    kv = _dot(h, wkv_ref[...])
    row = jnp.concatenate([_rope(kv[:, :KV_DIM], c, s1, s2), kv[:, KV_DIM:]], axis=1)
    kv_ref[...] = row
    kvb_ref[...] = row.astype(BF16)
    qi_ref[...] = (_rope(_dot(h, wqi_ref[...]), c, s1, s2) * IDX_SCALE).astype(BF16)
    ki = _rope(_dot(h, wki_ref[...]), c, s1, s2)[:, :IDX_DIM]
    ki_ref[...] = ki
    kib_ref[...] = ki.astype(BF16)
    wi_ref[...] = _dot(h, wwi_ref[...]) * N_IDX ** -0.5


def _dsa_proj(x, g, wq, wkv, wqi, wki, wwi, tabs, n_tab_blocks):
    n, d = x.shape
    tm = min(512, n)
    row = lambda w: pl.BlockSpec((tm, w), lambda i: (i, 0))
    full = lambda a: pl.BlockSpec(a.shape, lambda i: (0, 0))
    tab = pl.BlockSpec((tm, LANES), lambda i: (i % n_tab_blocks, 0))
    outs = [(Q_DIM, BF16), (KV_COLS, F32), (KV_COLS, BF16), (N_IDX * LANES, BF16),
            (IDX_DIM, F32), (IDX_DIM, BF16), (LANES, F32)]
    return pl.pallas_call(
        _dsa_proj_body,
        grid=(n // tm,),
        in_specs=[row(d), pl.BlockSpec((1, d), lambda i: (0, 0)), full(wq), full(wkv), full(wqi),
                  full(wki), full(wwi), tab, tab, tab],
        out_specs=[row(w) for w, _ in outs],
        out_shape=[jax.ShapeDtypeStruct((n, w), t) for w, t in outs],
        compiler_params=_params(("parallel",)),
        name="dsa_proj",
    )(x, g.reshape(1, d), wq, wkv, wqi, wki, wwi, *tabs)


def _compress_body(*refs, n_in, nblk, paged):
    if paged:
        refs = refs[1:]
    x_refs = refs[:n_in]
    pos_ref, w1_ref, w2_ref, o_ref = refs[n_in:n_in + 4]
    x_scr = refs[n_in + 4]
    rows = x_refs[0].shape[1]
    for p in range(n_in):
        for j in range(KV_COLS // LANES):
            x_scr[j, p * rows:(p + 1) * rows, :] = x_refs[p][0, :, j * LANES:(j + 1) * LANES]
    load = lambda l, col: x_scr[col // LANES, pl.ds(l, nblk, stride=BLOCK), :]
    outs = []
    for c in range(2):
        acc = jnp.zeros((N_KV * nblk, w1_ref.shape[-1]), F32)
        for l in range(BLOCK):
            pieces = []
            for kp in range(N_KV // 2):
                col = c * KV_DIM + kp * LANES
                a = (load(l, col) + pos_ref[l:l + 1, col:col + LANES]).astype(BF16)
                pieces += [a[:, :HEAD_DIM], a[:, HEAD_DIM:]]
            acc = acc + _dot(jnp.concatenate(pieces, axis=0), w1_ref[c, l])
        out = _dot(jax.nn.gelu(acc).astype(BF16), w2_ref[c])
        outs += [out[k * nblk:(k + 1) * nblk] for k in range(N_KV)]
    o_ref[...] = jnp.concatenate(outs, axis=1).astype(BF16)


def _compress_rows(x3, pos, w1, w2):
    g, r, _ = x3.shape
    nblk = r // BLOCK
    full = lambda a: pl.BlockSpec(a.shape, lambda i: (0,) * a.ndim)
    return pl.pallas_call(
        functools.partial(_compress_body, n_in=1, nblk=nblk, paged=False),
        grid=(g,),
        in_specs=[pl.BlockSpec((1, r, KV_COLS), lambda i: (i, 0, 0)), full(pos), full(w1), full(w2)],
        out_specs=pl.BlockSpec((nblk, KV_COLS), lambda i: (i, 0)),
        out_shape=jax.ShapeDtypeStruct((g * nblk, KV_COLS), BF16),
        scratch_shapes=[pltpu.VMEM((KV_COLS // LANES, r, LANES), F32)],
        compiler_params=_params(("parallel",)),
        name="compress_rows",
    )(x3, pos, w1, w2)


def _compress_paged_body(pt_ref, *refs, n_in):
    del pt_ref
    x_refs = refs[:n_in]
    pos_ref, w1_ref, w2_ref, o_ref, x_scr = refs[n_in:]
    per_page = N_KV * HEAD_DIM
    for p in range(n_in):
        for c in range(2):
            x_scr[c, p * per_page:(p + 1) * per_page, :] = x_refs[p][0, c].reshape(per_page, PAGE_SIZE)
    m = n_in * N_KV
    for c in range(2):
        acc = jnp.zeros((2 * m, w1_ref.shape[-1]), F32)
        for d in range(HEAD_DIM):
            a = (x_scr[c, pl.ds(d, m, stride=HEAD_DIM), :] + pos_ref[c, d:d + 1, :]).astype(BF16)
            acc = acc + _dot(jnp.concatenate([a[:, :BLOCK], a[:, BLOCK:]], axis=0), w1_ref[c, d])
        o_ref[0, c] = _dot(jax.nn.gelu(acc).astype(BF16), w2_ref[c]).astype(BF16)


def _compress_paged(pool, page_table, pos, w1, w2):
    db, n_pages = page_table.shape
    per_step = 2
    n_in = per_step * n_pages
    steps = db // per_step
    rows = 2 * n_in * N_KV
    full = lambda a: pl.BlockSpec(a.shape, lambda i, pt: (0,) * a.ndim)
    page = lambda p: pl.BlockSpec(
        (1, 2, N_KV, HEAD_DIM, PAGE_SIZE),
        lambda i, pt: (pt[per_step * i + p // n_pages, p % n_pages], 0, 0, 0, 0))
    out = pl.pallas_call(
        functools.partial(_compress_paged_body, n_in=n_in),
        grid_spec=pltpu.PrefetchScalarGridSpec(
            num_scalar_prefetch=1,
            grid=(steps,),
            in_specs=[page(p) for p in range(n_in)] + [full(pos), full(w1), full(w2)],
            out_specs=pl.BlockSpec((1, 2, rows, HEAD_DIM), lambda i, pt: (i, 0, 0, 0)),
            scratch_shapes=[pltpu.VMEM((2, n_in * N_KV * HEAD_DIM, PAGE_SIZE), F32)]),
        out_shape=jax.ShapeDtypeStruct((steps, 2, rows, HEAD_DIM), BF16),
        compiler_params=_params(("parallel",)),
        name="compress_paged",
    )(page_table, *([pool] * n_in), pos, w1, w2)
    out = out.reshape(steps, 2, 2, per_step, n_pages, N_KV, HEAD_DIM)
    return out.transpose(0, 3, 4, 2, 1, 5, 6).reshape(db * n_pages * 2, KV_COLS)


def _pair_rows(blk, half, tq):
    lane = _iota((1, LANES), 1)
    keep = (lane < HEAD_DIM) if half == 0 else (lane >= HEAD_DIM)
    zero = jnp.zeros((tq, LANES), blk.dtype)
    return jnp.concatenate(
        [jnp.where(keep, blk[:, g * LANES:(g + 1) * LANES], zero) for g in range(GROUP)], axis=0)


def _own_block():
    return jnp.right_shift(_iota((8, KV_DIM), 1), 6) == _iota((8, KV_DIM), 0)


def _head_rows(q_row):
    keep = _own_block()
    q32 = q_row.astype(F32)
    slabs = []
    for g in range(GROUP):
        qg = jnp.concatenate(
            [q32[:, (kp * GROUP + g) * LANES:(kp * GROUP + g + 1) * LANES] for kp in range(2)], axis=1)
        slabs.append(jnp.where(keep, jnp.broadcast_to(qg, (8, KV_DIM)), 0.0))
    return jnp.concatenate(slabs, axis=0).astype(q_row.dtype)


def _head_out(o_rows):
    keep = _own_block()
    per_g = [jnp.sum(jnp.where(keep, o_rows[g * 8:(g + 1) * 8], 0.0), axis=0, keepdims=True)
             for g in range(GROUP)]
    return jnp.concatenate(
        [per_g[g][:, kp * LANES:(kp + 1) * LANES] for kp in range(2) for g in range(GROUP)], axis=1)


def _select_top(score, n):
    idx = _iota(score.shape, 0).astype(F32)
    sel = jnp.zeros(score.shape, F32)
    for _ in range(n):
        m = jnp.max(score, axis=0, keepdims=True)
        j = jnp.min(jnp.where(score == m, idx, 1e9), axis=0, keepdims=True)
        hit = idx == j
        sel = jnp.where(hit, jnp.where(m >= 0.0, 1.0, 0.0), sel)
        score = jnp.where(hit, -2.0, score)
    return sel


def _softmax_masked_cols(s, mask):
    s = jnp.where(mask, s, NEG)
    p = jnp.where(mask, jnp.exp(s - jnp.max(s, axis=0, keepdims=True)), 0.0)
    return p / jnp.maximum(jnp.sum(p, axis=0, keepdims=True), 1e-30)


def _block_scores(p_t, qpos_l, n_q):
    nb = p_t.shape[0]
    imp = p_t[:, 0:n_q]
    for g in range(1, GROUP):
        imp = imp + p_t[:, g * n_q:(g + 1) * n_q]
    blk = _iota((nb, 1), 0)
    cur = jnp.right_shift(qpos_l, 6)
    forced = (blk == 0) | (blk == cur) | (blk == cur - 1)
    score = jnp.where(forced, FORCE, imp)
    return jnp.where(blk <= cur, score, -1.0)


def _online_attend(q_rows, get_k, get_v, get_bias, n_chunks, kt):
    r = q_rows.shape[0]

    def body(c, carry):
        m, l, acc = carry
        k, v = get_k(c), get_v(c)
        s = (_dot(q_rows, k) if kt else _dot_nt(q_rows, k)) + get_bias(c)
        m_new = jnp.maximum(m, jnp.max(s, axis=-1, keepdims=True))
        alpha = jnp.exp(m - m_new)
        p = jnp.exp(s - m_new)
        l = alpha * l + jnp.sum(p, axis=-1, keepdims=True)
        pb = p.astype(BF16)
        acc = alpha * acc + (_dot_nt(pb, v) if kt else _dot(pb, v))
        return m_new, l, acc

    init = (jnp.full((r, 1), NEG, F32), jnp.zeros((r, 1), F32), jnp.zeros((r, LANES), F32))
    if isinstance(n_chunks, int):
        carry = init
        for c in range(n_chunks):
            carry = body(c, carry)
        _, l, acc = carry
    else:
        _, l, acc = lax.fori_loop(0, n_chunks, body, init)
    return acc / jnp.maximum(l, 1e-30)


def _nsa_core(q_blk, qr_blk, gates, kc, vc, get_k, get_v, n_chunks, ck, wk, wv, wstart, qpos_t,
              qpos_l):
    tq = q_blk.shape[0]
    nb = kc.shape[0]
    wlen = wk.shape[0]
    lane = _iota((1, LANES), 1)
    qpos_r = _tile_rows(qpos_t, GROUP)
    qpos_lr = jnp.concatenate([qpos_l] * GROUP, axis=1)
    blk = _iota((nb, 1), 0)
    per_half = []
    for half in range(2):
        q_rows = _pair_rows(q_blk, half, tq)
        qr_rows = _pair_rows(qr_blk, half, tq)
        cmask = ((blk + 1) * BLOCK - 1) <= qpos_lr
        p_t = _softmax_masked_cols(_dot_nt(kc, q_rows), cmask)
        o_cmp = _dot(p_t.T.astype(BF16), vc)
        sel = _select_top(_block_scores(p_t, qpos_l, tq), min(N_SEL, nb)).T.astype(BF16)

        def slc_bias(c, sel=sel):
            kpos = c * ck + _iota((1, ck), 1)
            expand = jnp.where(jnp.right_shift(kpos, 6) == blk, 1.0, 0.0).astype(BF16)
            chosen = _dot(sel, expand)
            bias = jnp.where((chosen > 0.5) & (kpos <= qpos_t), 0.0, NEG)
            return _tile_rows(bias, GROUP)

        o_slc = _online_attend(qr_rows, get_k, get_v, slc_bias, n_chunks, False)
        kwpos = wstart + _iota((1, wlen), 1)
        dpos = qpos_r - kwpos
        wmask = (dpos >= 0) & (dpos <= WINDOW) & (kwpos >= 0)
        o_win = _dot(_softmax_masked(_dot_nt(qr_rows, wk), wmask).astype(BF16), wv)
        outs = []
        for g in range(GROUP):
            c0 = half * GROUP * 3 + g * 3
            sl = slice(g * tq, (g + 1) * tq)
            outs.append(gates[:, c0:c0 + 1] * o_cmp[sl] + gates[:, c0 + 1:c0 + 2] * o_slc[sl]
                        + gates[:, c0 + 2:c0 + 3] * o_win[sl])
        per_half.append(outs)
    return jnp.concatenate(
        [jnp.where(lane < HEAD_DIM, per_half[0][g], per_half[1][g]) for g in range(GROUP)], axis=1)


def _nsa_prompt_body(q_ref, qr_ref, gt_ref, kc_ref, vc_ref, sk_ref, sv_ref, wk_ref, wv_ref, o_ref,
                     *, tq, ck):
    s0 = pl.program_id(2) * tq
    qpos_t = s0 + _iota((tq, 1), 0)
    n_chunks = (s0 + tq + ck - 1) // ck
    wstart = pl.multiple_of(jnp.maximum(s0 - WINDOW, 0), tq)
    chunk = lambda ref: (lambda c: ref[pl.ds(pl.multiple_of(c * ck, ck), ck), :])
    out = _nsa_core(q_ref[...], qr_ref[...], gt_ref[...], kc_ref[...], vc_ref[...],
                    chunk(sk_ref), chunk(sv_ref), n_chunks, ck,
                    wk_ref[pl.ds(wstart, WINDOW + tq), :], wv_ref[pl.ds(wstart, WINDOW + tq), :],
                    wstart, qpos_t, s0 + _iota((1, tq), 1))
    o_ref[...] = out.astype(BF16)


def _nsa_attn_prompt(q, qr, gates, kvc, slc_b, win_b, b, t):
    tq = 128
    ck = min(512, t)
    nt = t // tq
    nb = t // BLOCK
    qspec = pl.BlockSpec((tq, 4 * LANES), lambda bi, kp, i: (bi * nt + i, kp))
    seq = lambda off: pl.BlockSpec((t, LANES), lambda bi, kp, i: (bi, kp + off))
    return pl.pallas_call(
        functools.partial(_nsa_prompt_body, tq=tq, ck=ck),
        grid=(b, 2, nt),
        in_specs=[qspec, qspec,
                  pl.BlockSpec((tq, LANES), lambda bi, kp, i: (bi * nt + i, kp)),
                  pl.BlockSpec((nb, LANES), lambda bi, kp, i: (bi, kp)),
                  pl.BlockSpec((nb, LANES), lambda bi, kp, i: (bi, kp + 2)),
                  seq(0), seq(2), seq(0), seq(2)],
        out_specs=qspec,
        out_shape=jax.ShapeDtypeStruct((b * t, Q_DIM), BF16),
        compiler_params=_params(("parallel", "parallel", "arbitrary")),
        name="nsa_attn_prompt",
    )(q, qr, gates, kvc, kvc, slc_b, slc_b, win_b, win_b)


def _tail_rows(row, n, dtype):
    w = row.shape[-1]
    first = _iota((n, w), 0) == 0
    return jnp.where(first, jnp.broadcast_to(row, (n, w)), 0.0).astype(dtype)


def _tail_cols(row, n):
    return _tail_rows(row, n, F32).T.reshape(2, N_KV, HEAD_DIM, n).astype(BF16)


def _feature_major(a):
    return jnp.moveaxis(a, -4, -1)


def _nsa_sample_body(pt_ref, q_ref, qr_ref, gt_ref, kcp_ref, kcn_ref, *rest,
                     n_pages, n_past, wb, tq, tail):
    del pt_ref
    pages = rest[:n_pages]
    slcn_ref, wst_ref, winn_ref, o_ref, kc_scr, slc_scr, win_scr = rest[n_pages:]
    nbp = kcp_ref.shape[0]
    kc_scr[...] = jnp.zeros_like(kc_scr)
    kc_scr[0:nbp, :] = kcp_ref[...]
    kc_scr[nbp:nbp + 16, :] = _tail_rows(kcn_ref[0].astype(F32), 16, BF16)
    for p in range(n_pages):
        slc_scr[:, :, :, p * PAGE_SIZE:(p + 1) * PAGE_SIZE] = pages[p][0].astype(BF16)
    slc_scr[:, :, :, n_past:n_past + tail] = _tail_cols(slcn_ref[0], tail)
    win_scr[:, :, :, 0:wb] = wst_ref[0].astype(BF16)
    win_scr[:, :, :, wb:wb + tail] = _tail_cols(winn_ref[0], tail)
    nb = kc_scr.shape[0]
    n_keys = n_past + tail
    q32, qr32 = _head_rows(q_ref[0]), _head_rows(qr_ref[0])
    blk = _iota((nb, 1), 0)
    gap = jnp.zeros((LANES - 8, KV_DIM), BF16)
    q_cols = jnp.concatenate(
        [x for g in range(GROUP) for x in (q32[g * 8:(g + 1) * 8], gap)], axis=0)
    cmask = ((blk + 1) * BLOCK - 1) <= n_past
    p_t = _softmax_masked_cols(_dot_nt(kc_scr[:, 0:KV_DIM], q_cols), cmask)
    p_rows = p_t.T
    p32 = jnp.concatenate([p_rows[g * LANES:g * LANES + 8] for g in range(GROUP)], axis=0)
    o_cmp = _dot(p32.astype(BF16), kc_scr[:, KV_DIM:KV_COLS])
    qpos_l = jnp.full((1, LANES), n_past, I32)
    sel8 = _select_top(_block_scores(p_t, qpos_l, LANES), min(N_SEL, nb)).T[0:8]
    sel32 = _tile_rows(sel8, GROUP).astype(BF16)
    kpos = _iota((1, n_keys), 1)
    expand = jnp.where(jnp.right_shift(kpos, 6) == blk, 1.0, 0.0).astype(BF16)
    smask = (_dot(sel32, expand) > 0.5) & (kpos <= n_past)
    p = _softmax_masked(_dot(qr32, slc_scr[0].reshape(KV_DIM, n_keys)), smask)
    o_slc = _dot_nt(p.astype(BF16), slc_scr[1].reshape(KV_DIM, n_keys))
    wlen = wb + tail
    kwpos = (n_past - wb) + _iota((1, wlen), 1)
    dpos = n_past - kwpos
    wmask = (dpos >= 0) & (dpos <= WINDOW) & (kwpos >= 0)
    p = _softmax_masked(_dot(qr32, win_scr[0].reshape(KV_DIM, wlen)), wmask)
    o_win = _dot_nt(p.astype(BF16), win_scr[1].reshape(KV_DIM, wlen))
    rows = q32.shape[0]
    r = _iota((rows, 2 * LANES), 0)
    k, g = r & 7, jnp.right_shift(r, 3)
    base = jnp.right_shift(k, 1) * LANES + (k & 1) * (GROUP * 3) + g * 3
    col = _iota((rows, 2 * LANES), 1)
    gb = jnp.broadcast_to(gt_ref[0], (rows, 2 * LANES))
    gate = lambda br: jnp.sum(jnp.where(col == base + br, gb, 0.0), axis=-1, keepdims=True)
    o_ref[0] = _head_out(gate(0) * o_cmp + gate(1) * o_slc + gate(2) * o_win).astype(BF16)


def _nsa_attn_sample(q, qr, gates, kvc_past, kvc_new, cache_slc, page_table, slc_new, win_state,
                     win_new):
    db, n_pages = page_table.shape
    n_past = n_pages * PAGE_SIZE
    wb = win_state.shape[-1]
    nbp = n_past // BLOCK
    tq, tail = 8, PAGE_SIZE
    nb = 64
    assert nb * BLOCK >= n_past + tail and nbp + 16 <= nb
    one = lambda w: pl.BlockSpec((1, 1, w), lambda i, pt: (i, 0, 0))
    fm = lambda n: (1, 2, N_KV, HEAD_DIM, n)
    page = lambda p: pl.BlockSpec(fm(PAGE_SIZE), lambda i, pt: (pt[i, p], 0, 0, 0, 0))
    r3 = lambda a: a.reshape(db, 1, a.shape[-1])
    return pl.pallas_call(
        functools.partial(_nsa_sample_body, n_pages=n_pages, n_past=n_past, wb=wb, tq=tq, tail=tail),
        grid_spec=pltpu.PrefetchScalarGridSpec(
            num_scalar_prefetch=1,
            grid=(db,),
            in_specs=[one(Q_DIM), one(Q_DIM), one(2 * LANES),
                      pl.BlockSpec((nbp, KV_COLS), lambda i, pt: (i, 0)), one(KV_COLS)]
                     + [page(p) for p in range(n_pages)]
                     + [one(KV_COLS), pl.BlockSpec(fm(wb), lambda i, pt: (i, 0, 0, 0, 0)),
                        one(KV_COLS)],
            out_specs=one(Q_DIM),
            scratch_shapes=[pltpu.VMEM((nb, KV_COLS), BF16),
                            pltpu.VMEM(fm(n_past + tail)[1:], BF16),
                            pltpu.VMEM(fm(wb + tail)[1:], BF16)]),
        out_shape=jax.ShapeDtypeStruct((db, 1, Q_DIM), BF16),
        compiler_params=_params(("parallel",)),
        name="nsa_attn_sample",
    )(page_table, r3(q), r3(qr), r3(gates), kvc_past, r3(kvc_new), *([cache_slc] * n_pages),
      r3(slc_new), win_state, r3(win_new)).reshape(db, Q_DIM)


def _dsa_select(qi_blk, wi, get_ki, n_chunks, ck, qpos_t, key_scr, topk, kt, same_rows=False):
    tq = qi_blk.shape[0]
    qi_rows = jnp.concatenate(
        [qi_blk[:, h * LANES:h * LANES + IDX_DIM] for h in range(N_IDX)], axis=0)
    wcol = [wi[:, h:h + 1] for h in range(N_IDX)]

    def loop(fn, init):
        if isinstance(n_chunks, int):
            for c in range(n_chunks):
                init = fn(c, init)
            return init
        return lax.fori_loop(0, n_chunks, fn, init)

    def kpos_of(c):
        return c * ck + _iota((1, ck), 1)

    def score_chunk(c, carry):
        sc = _dot(qi_rows, get_ki(c)) if kt else _dot_nt(qi_rows, get_ki(c))
        score = jnp.maximum(sc[0:tq], 0.0) * wcol[0]
        for h in range(1, N_IDX):
            score = score + jnp.maximum(sc[h * tq:(h + 1) * tq], 0.0) * wcol[h]
        score = jnp.where(kpos_of(c) <= qpos_t, score, NEG)
        score = jnp.where(score == 0.0, 0.0, score)
        bits = lax.bitcast_convert_type(score, I32)
        key_scr[c] = jnp.where(bits < 0, bits ^ 0x7FFFFFFF, bits)
        return carry

    loop(score_chunk, 0)

    def count(pred):
        def fn(c, acc):
            hit = jnp.where(pred(key_scr[c], kpos_of(c)), 1.0, 0.0)
            part = hit[:, 0:LANES]
            for j in range(1, ck // LANES):
                part = part + hit[:, j * LANES:(j + 1) * LANES]
            return acc + part
        return jnp.sum(loop(fn, jnp.zeros((tq, LANES), F32)), axis=-1, keepdims=True)

    kf = float(topk)
    if same_rows:
        assert tq == 8
        row = _iota((tq, 1), 0)
        digit = jnp.minimum(row + 1, 7)

        def digit_step(it, lo):
            shift = jnp.maximum(29 - 3 * it, 0)
            cand = lo + jnp.left_shift(digit, shift)
            ok = (count(lambda k, p: k >= cand) >= kf) & (row < 7)
            kept = jnp.sum(jnp.where(ok, 1.0, 0.0), axis=0, keepdims=True).astype(I32)
            return lo + jnp.left_shift(kept, shift)

        thr = lax.fori_loop(0, 11, digit_step, jnp.full((tq, 1), INT_MIN, I32))
    else:
        lo = jnp.where(count(lambda k, p: k >= 0) >= kf, 0, INT_MIN).astype(I32)

        def bit_step(it, lo):
            cand = lo + jnp.left_shift(jnp.int32(1), 30 - it)
            return jnp.where(count(lambda k, p: k >= cand) >= kf, cand, lo)

        thr = lax.fori_loop(0, 31, bit_step, lo)
    need = kf - count(lambda k, p: k > thr)
    surplus = jnp.max(count(lambda k, p: k >= thr)) > kf

    def idx_step(it, lo):
        mid = lo + jnp.left_shift(jnp.int32(1), 13 - it)
        enough = count(lambda k, p: (k == thr) & (p <= mid)) >= need
        return jnp.where(enough, lo, mid)

    jmax = lax.cond(
        surplus,
        lambda: lax.fori_loop(0, 14, idx_step, jnp.full((tq, 1), -1, I32)) + 1,
        lambda: jnp.full((tq, 1), 2 ** 14, I32))

    def bias_chunk(c, carry):
        k = key_scr[c]
        p = kpos_of(c)
        keep = ((k > thr) | ((k == thr) & (p <= jmax))) & (p <= qpos_t)
        key_scr[c] = lax.bitcast_convert_type(jnp.where(keep, 0.0, NEG), I32)
        return carry

    loop(bias_chunk, 0)


def _dsa_prompt_body(q_ref, qi_ref, wi_ref, ki_ref, k0_ref, k1_ref, v0_ref, v1_ref, o_ref, key_scr,
                     *, tq, ck, topk):
    s0 = pl.program_id(1) * tq
    qpos_t = s0 + _iota((tq, 1), 0)
    n_chunks = (s0 + tq + ck - 1) // ck
    rows = lambda c: pl.ds(pl.multiple_of(c * ck, ck), ck)
    k_refs, v_refs = (k0_ref, k1_ref), (v0_ref, v1_ref)
    _dsa_select(qi_ref[...], wi_ref[...], lambda c: ki_ref[rows(c), :], n_chunks, ck, qpos_t, key_scr,
                topk, False)
    get_bias = lambda c: _tile_rows(lax.bitcast_convert_type(key_scr[c], F32), GROUP)
    q_blk = q_ref[...]
    lane = _iota((1, LANES), 1)
    for kp in range(2):
        per_half = []
        for half in range(2):
            q_rows = _pair_rows(q_blk[:, kp * 4 * LANES:(kp + 1) * 4 * LANES], half, tq)
            per_half.append(_online_attend(q_rows, lambda c, kp=kp: k_refs[kp][rows(c), :],
                                           lambda c, kp=kp: v_refs[kp][rows(c), :], get_bias,
                                           n_chunks, False))
        for g in range(GROUP):
            sl = slice(g * tq, (g + 1) * tq)
            cols = slice((kp * GROUP + g) * LANES, (kp * GROUP + g + 1) * LANES)
            o_ref[:, cols] = jnp.where(lane < HEAD_DIM, per_half[0][sl], per_half[1][sl]).astype(BF16)


def _dsa_attn_prompt(q, qi, wi, ki_b, kv_b, b, t):
    tq = 128
    ck = min(512, t)
    nt = t // tq
    row = lambda w: pl.BlockSpec((tq, w), lambda bi, i: (bi * nt + i, 0))
    seq = lambda j: pl.BlockSpec((t, LANES), lambda bi, i: (bi, j))
    return pl.pallas_call(
        functools.partial(_dsa_prompt_body, tq=tq, ck=ck, topk=min(DSA_TOPK, t // 4)),
        grid=(b, nt),
        in_specs=[row(Q_DIM), row(N_IDX * LANES), row(LANES),
                  pl.BlockSpec((t, IDX_DIM), lambda bi, i: (bi, 0)),
                  seq(0), seq(1), seq(2), seq(3)],
        out_specs=row(Q_DIM),
        out_shape=jax.ShapeDtypeStruct((b * t, Q_DIM), BF16),
        scratch_shapes=[pltpu.VMEM((t // ck, tq, ck), I32)],
        compiler_params=_params(("parallel", "arbitrary")),
        name="dsa_attn_prompt",
    )(q, qi, wi, ki_b, kv_b, kv_b, kv_b, kv_b)


def _dsa_sample_body(pt_ref, q_ref, qi_ref, wi_ref, *rest, n_pages, n_past, tq, tail, topk):
    del pt_ref
    kv_pages = rest[:n_pages]
    ki_pages = rest[n_pages:2 * n_pages]
    kvn_ref, kin_ref, o_ref, kv_scr, ki_scr, key_scr = rest[2 * n_pages:]
    for p in range(n_pages):
        cols = slice(p * PAGE_SIZE, (p + 1) * PAGE_SIZE)
        kv_scr[:, :, :, cols] = kv_pages[p][0].astype(BF16)
        ki_scr[:, cols] = ki_pages[p][0].astype(BF16)
    kv_scr[:, :, :, n_past:n_past + tail] = _tail_cols(kvn_ref[0], tail)
    ki_scr[:, n_past:n_past + tail] = _tail_rows(kin_ref[0], tail, F32).T[0:IDX_DIM].astype(BF16)
    qpos_t = jnp.full((tq, 1), n_past, I32)
    ck = n_past + tail
    _dsa_select(jnp.broadcast_to(qi_ref[0], (tq, N_IDX * LANES)), jnp.broadcast_to(wi_ref[0], (tq, LANES)),
                lambda c: ki_scr[...], 1, ck, qpos_t, key_scr, topk, True, same_rows=True)
    mask = lax.bitcast_convert_type(key_scr[0, 0:1, :], F32) == 0.0
    p = _softmax_masked(_dot(_head_rows(q_ref[0]), kv_scr[0].reshape(KV_DIM, ck)), mask)
    o_ref[0] = _head_out(_dot_nt(p.astype(BF16), kv_scr[1].reshape(KV_DIM, ck))).astype(BF16)


def _dsa_attn_sample(q, qi, wi, cache_kv, cache_idx, page_table, kv_new, ki_new):
    db, n_pages = page_table.shape
    n_past = n_pages * PAGE_SIZE
    tq, tail = 8, PAGE_SIZE
    one = lambda w: pl.BlockSpec((1, 1, w), lambda i, pt: (i, 0, 0))
    kv_page = lambda p: pl.BlockSpec((1, 2, N_KV, HEAD_DIM, PAGE_SIZE),
                                     lambda i, pt: (pt[i, p], 0, 0, 0, 0))
    ki_page = lambda p: pl.BlockSpec((1, IDX_DIM, PAGE_SIZE), lambda i, pt: (pt[i, p], 0, 0))
    r3 = lambda a: a.reshape(db, 1, a.shape[-1])
    return pl.pallas_call(
        functools.partial(_dsa_sample_body, n_pages=n_pages, n_past=n_past, tq=tq, tail=tail,
                          topk=min(DSA_TOPK, (n_past + 1) // 4)),
        grid_spec=pltpu.PrefetchScalarGridSpec(
            num_scalar_prefetch=1,
            grid=(db,),
            in_specs=[one(Q_DIM), one(N_IDX * LANES), one(LANES)]
                     + [kv_page(p) for p in range(n_pages)]
                     + [ki_page(p) for p in range(n_pages)]
                     + [one(KV_COLS), one(LANES)],
            out_specs=one(Q_DIM),
            scratch_shapes=[pltpu.VMEM((2, N_KV, HEAD_DIM, n_past + tail), BF16),
                            pltpu.VMEM((IDX_DIM, n_past + tail), BF16),
                            pltpu.VMEM((1, tq, n_past + tail), I32)]),
        out_shape=jax.ShapeDtypeStruct((db, 1, Q_DIM), BF16),
        compiler_params=_params(("parallel",)),
        name="dsa_attn_sample",
    )(page_table, r3(q), r3(qi), r3(wi), *([cache_kv] * n_pages), *([cache_idx] * n_pages),
      r3(kv_new), r3(ki_new)).reshape(db, Q_DIM)


def _rope_tables(pos):
    half = ROT_DIM // 2
    n = pos.shape[0]
    inv = ROPE_THETA ** (-jnp.arange(half, dtype=F32) * 2.0 / ROT_DIM)
    ang = pos.astype(F32)[:, None] * inv[None, :]
    cos, sin = jnp.cos(ang), jnp.sin(ang)
    z8 = jnp.zeros((n, half), F32)
    rest = HEAD_DIM - ROT_DIM
    c = jnp.concatenate([cos, cos, jnp.ones((n, rest), F32)], axis=1)
    s1 = jnp.concatenate([-sin, z8, jnp.zeros((n, rest), F32)], axis=1)
    s2 = jnp.concatenate([z8, sin, jnp.zeros((n, rest), F32)], axis=1)
    rep = LANES // HEAD_DIM
    return tuple(jnp.tile(a, (1, rep)) for a in (c, s1, s2))


def _pair_perm():
    cols = []
    for kp in range(N_KV // 2):
        for g in range(GROUP):
            for half in range(2):
                h = (2 * kp + half) * GROUP + g
                cols.extend(range(h * HEAD_DIM, (h + 1) * HEAD_DIM))
    return jnp.array(cols, dtype=I32)


def _pad_cols(w, n):
    return jnp.pad(w, ((0, 0), (0, n - w.shape[1])))


def _nsa_layer(xp, xs, g, w_in, w_out, cmp_pos, cmp_w1, cmp_w2, cache_cmp, cache_slc, win_state,
               page_table, tabs_p, tabs_s, b, t):
    db = xs.shape[0]
    perm = _pair_perm()
    wq = w_in[:, :Q_DIM][:, perm].astype(BF16)
    wkv = w_in[:, Q_DIM:Q_DIM + 3 * KV_COLS].astype(BF16)
    wg = w_in[:, Q_DIM + 3 * KV_COLS:]
    wgt = jnp.concatenate([_pad_cols(wg[:, kp * 2 * GROUP * 3:(kp + 1) * 2 * GROUP * 3], LANES)
                           for kp in range(N_KV // 2)], axis=1).astype(BF16)
    wo = w_out[perm, :].astype(BF16)
    pos = jnp.concatenate([jnp.tile(cmp_pos[:, c, :], (1, N_KV)) for c in range(2)], axis=1)
    w1 = cmp_w1.astype(BF16)
    w2 = cmp_w2.astype(BF16)

    q, qr, gates, kcmp, kslc, kwin, slc_b, win_b = _nsa_proj(xp, g, wq, wkv, wgt, tabs_p, t // min(512, t))
    kvc = _compress_rows(kcmp.reshape(b, t, KV_COLS), pos, w1, w2)
    o = _nsa_attn_prompt(q, qr, gates, kvc, slc_b, win_b, b, t)
    xp = _oproj(xp, o, wo)
    qs, qrs, gs, scmp, sslc, swin, _, _ = _nsa_proj(xs, g, wq, wkv, wgt, tabs_s, 1)
    pos_fm = jnp.tile(cmp_pos.transpose(1, 2, 0), (1, 1, PAGE_SIZE // BLOCK))
    w1_fm = cmp_w1.transpose(0, 2, 1, 3).astype(BF16)
    kvc_past = _compress_paged(_feature_major(cache_cmp), page_table, pos_fm, w1_fm, w2)
    new_rows = jnp.pad(scmp[:, None, :], ((0, 0), (0, BLOCK - 1), (0, 0)))
    kvc_new = _compress_rows(new_rows.reshape(2, db // 2 * BLOCK, KV_COLS), pos, w1, w2)
    os_ = _nsa_attn_sample(qs, qrs, gs, kvc_past, kvc_new, _feature_major(cache_slc), page_table,
                           sslc, _feature_major(win_state), swin)
    xs = _oproj(xs, os_, wo)
    wb = min(WINDOW, t)
    kv5 = lambda a, n: a.reshape(n, -1, 2, N_KV, HEAD_DIM)
    win_s = jnp.concatenate([win_state, kv5(swin, db)], axis=1)[:, 1:]
    outs = (kv5(kcmp, b), kv5(scmp, db), kv5(kslc, b), kv5(sslc, db),
            kv5(kwin, b)[:, t - wb:], win_s)
    return xp, xs, outs


def _dsa_layer(xp, xs, g, w_in, w_out, cache_kv, cache_idx, page_table, tabs_p, tabs_s, b, t):
    db = xs.shape[0]
    perm = _pair_perm()
    wq = w_in[:, :Q_DIM][:, perm].astype(BF16)
    o1 = Q_DIM + KV_COLS
    wkv = w_in[:, Q_DIM:o1].astype(BF16)
    o2 = o1 + N_IDX * IDX_DIM
    wqi = jnp.concatenate([_pad_cols(w_in[:, o1 + h * IDX_DIM:o1 + (h + 1) * IDX_DIM], LANES)
                           for h in range(N_IDX)], axis=1).astype(BF16)
    wki = _pad_cols(w_in[:, o2:o2 + IDX_DIM], LANES).astype(BF16)
    wwi = _pad_cols(w_in[:, o2 + IDX_DIM:], LANES).astype(BF16)
    wo = w_out[perm, :].astype(BF16)

    q, kv, kv_b, qi, ki, ki_b, wi = _dsa_proj(xp, g, wq, wkv, wqi, wki, wwi, tabs_p, t // min(512, t))
    o = _dsa_attn_prompt(q, qi, wi, ki_b, kv_b, b, t)
    xp = _oproj(xp, o, wo)
    qs, kvs, _, qis, kis, _, wis = _dsa_proj(xs, g, wq, wkv, wqi, wki, wwi, tabs_s, 1)
    os_ = _dsa_attn_sample(qs, qis, wis, _feature_major(cache_kv), jnp.swapaxes(cache_idx, -1, -2),
                           page_table, kvs, _pad_cols(kis, LANES))
    xs = _oproj(xs, os_, wo)
    outs = (kv.reshape(b, t, 2, N_KV, HEAD_DIM), kvs.reshape(db, 1, 2, N_KV, HEAD_DIM),
            ki.reshape(b, t, IDX_DIM), kis.reshape(db, 1, IDX_DIM))
    return xp, xs, outs


def kernel(x_prompt, x_sample, cache_nsa_cmp, cache_nsa_slc, state_nsa_win, cache_dsa_kv, cache_dsa_idx,
           page_table, norm_g, w_ffn_in, w_ffn_out, w_nsa_in, w_nsa_out, nsa_cmp_pos, nsa_cmp_w1,
           nsa_cmp_w2, w_dsa_in, w_dsa_out, final_norm_g):
    b, t, d = x_prompt.shape
    db, ds, _ = x_sample.shape
    assert ds == 1
    depth = norm_g.shape[0]
    n_past = page_table.shape[1] * PAGE_SIZE
    xp = x_prompt.reshape(b * t, d)
    xs = x_sample.reshape(db, d)
    tabs_p = _rope_tables(jnp.arange(t))
    tabs_s = _rope_tables(jnp.full((db,), n_past, I32))
    w_in_b = w_ffn_in.astype(BF16)
    w_out_b = w_ffn_out.astype(BF16)
    nsa_outs, dsa_outs = [], []
    for layer in range(depth):
        kind, j = layer % 2, layer // 2
        xp = _ffn(xp, norm_g[layer, 0], w_in_b[layer, 0], w_out_b[layer, 0])
        xs = _ffn(xs, norm_g[layer, 0], w_in_b[layer, 0], w_out_b[layer, 0])
        if kind == 0:
            xp, xs, outs = _nsa_layer(xp, xs, norm_g[layer, 1], w_nsa_in[j], w_nsa_out[j],
                                      nsa_cmp_pos[j], nsa_cmp_w1[j], nsa_cmp_w2[j], cache_nsa_cmp[j],
                                      cache_nsa_slc[j], state_nsa_win[j], page_table, tabs_p, tabs_s,
                                      b, t)
            nsa_outs.append(outs)
        else:
            xp, xs, outs = _dsa_layer(xp, xs, norm_g[layer, 1], w_dsa_in[j], w_dsa_out[j],
                                      cache_dsa_kv[j], cache_dsa_idx[j], page_table, tabs_p, tabs_s,
                                      b, t)
            dsa_outs.append(outs)
        fg = final_norm_g if layer == depth - 1 else None
        xp = _ffn(xp, norm_g[layer, 2], w_in_b[layer, 1], w_out_b[layer, 1], fg)
        xs = _ffn(xs, norm_g[layer, 2], w_in_b[layer, 1], w_out_b[layer, 1], fg)
    stack = lambda outs, k: jnp.stack([o[k] for o in outs])
    return (xp.reshape(b, t, d), xs.reshape(db, ds, d),
            stack(nsa_outs, 0), stack(nsa_outs, 1), stack(nsa_outs, 2), stack(nsa_outs, 3),
            stack(nsa_outs, 4), stack(nsa_outs, 5),
            stack(dsa_outs, 0), stack(dsa_outs, 1), stack(dsa_outs, 2), stack(dsa_outs, 3))
```

```python
import functools

import jax
import jax.numpy as jnp
from jax import lax
from jax.experimental import pallas as pl
from jax.experimental.pallas import tpu as pltpu

F32 = jnp.float32
BF16 = jnp.bfloat16
I32 = jnp.int32

D_MODEL = 1024
N_HEADS = 16
HEAD_DIM = 64
N_KV = 4
GROUP = N_HEADS // N_KV
Q_DIM = N_HEADS * HEAD_DIM
KV_DIM = N_KV * HEAD_DIM
ROT_DIM = HEAD_DIM // 4
ROPE_THETA = 500000.0
D_FF = 2816
BLOCK = 64
N_SEL = 8
WINDOW = 512
FORCE = 1.0e4
N_IDX = 8
IDX_DIM = 64
DSA_TOPK = 256
PAGE_SIZE = 128
RMS_EPS = 1e-6
NEG = -1e30

LANES = 128
KV_COLS = 2 * KV_DIM
Q_SCALE = HEAD_DIM ** -0.5
IDX_SCALE = IDX_DIM ** -0.5
INT_MIN = -(2 ** 31)
VMEM_LIMIT = 56 * 2 ** 20


def _params(sem):
    return pltpu.CompilerParams(dimension_semantics=sem, vmem_limit_bytes=VMEM_LIMIT)


def _dot(a, b):
    return jnp.dot(a, b, preferred_element_type=F32)


def _dot_nt(a, b):
    return lax.dot_general(a, b, (((1,), (1,)), ((), ())), preferred_element_type=F32)


def _iota(shape, dim):
    return lax.broadcasted_iota(I32, shape, dim)


def _rms(x, g):
    return x * lax.rsqrt(jnp.mean(x * x, axis=-1, keepdims=True) + RMS_EPS) * g


def _rope(x, c, s1, s2):
    half = ROT_DIM // 2
    outs = []
    for j in range(x.shape[1] // LANES):
        xs = x[:, j * LANES:(j + 1) * LANES]
        outs.append(xs * c + pltpu.roll(xs, LANES - half, 1) * s1 + pltpu.roll(xs, half, 1) * s2)
    return outs[0] if len(outs) == 1 else jnp.concatenate(outs, axis=1)


def _softmax_masked(s, mask):
    s = jnp.where(mask, s, NEG)
    p = jnp.where(mask, jnp.exp(s - jnp.max(s, axis=-1, keepdims=True)), 0.0)
    return p / jnp.maximum(jnp.sum(p, axis=-1, keepdims=True), 1e-30)


def _tile_rows(x, n):
    return jnp.concatenate([x] * n, axis=0)


def _ffn_body(*refs, n_ff, final):
    if final:
        x_ref, g_ref, wg_ref, wu_ref, wo_ref, fg_ref, o_ref, h_scr, acc_scr = refs
    else:
        x_ref, g_ref, wg_ref, wu_ref, wo_ref, o_ref, h_scr, acc_scr = refs
    j = pl.program_id(1)

    @pl.when(j == 0)
    def _():
        h_scr[...] = _rms(x_ref[...], g_ref[...]).astype(BF16)
        acc_scr[...] = jnp.zeros_like(acc_scr)

    h = h_scr[...]
    gate = _dot(h, wg_ref[...])
    up = _dot(h, wu_ref[...])
    act = (gate * jax.nn.sigmoid(gate) * up).astype(BF16)
    acc_scr[...] += _dot(act, wo_ref[...])

    @pl.when(j == n_ff - 1)
    def _():
        y = x_ref[...] + 0.5 * acc_scr[...]
        if final:
            y = _rms(y, fg_ref[...])
        o_ref[...] = y


def _ffn(x, g, w_in, w_out, final_g=None):
    n, d = x.shape
    tm = min(512, n)
    n_ff = 2
    fc = D_FF // n_ff
    in_specs = [
        pl.BlockSpec((tm, d), lambda i, j: (i, 0)),
        pl.BlockSpec((1, d), lambda i, j: (0, 0)),
        pl.BlockSpec((d, fc), lambda i, j: (0, j)),
        pl.BlockSpec((d, fc), lambda i, j: (0, j + n_ff)),
        pl.BlockSpec((fc, d), lambda i, j: (j, 0)),
    ]
    args = [x, g.reshape(1, d), w_in, w_in, w_out]
    if final_g is not None:
        in_specs.append(pl.BlockSpec((1, d), lambda i, j: (0, 0)))
        args.append(final_g.reshape(1, d))
    return pl.pallas_call(
        functools.partial(_ffn_body, n_ff=n_ff, final=final_g is not None),
        grid=(n // tm, n_ff),
        in_specs=in_specs,
        out_specs=pl.BlockSpec((tm, d), lambda i, j: (i, 0)),
        out_shape=jax.ShapeDtypeStruct((n, d), F32),
        scratch_shapes=[pltpu.VMEM((tm, d), BF16), pltpu.VMEM((tm, d), F32)],
        compiler_params=_params(("parallel", "arbitrary")),
        name="ffn",
    )(*args)


def _oproj_body(x_ref, o_ref, w_ref, y_ref):
    y_ref[...] = x_ref[...] + _dot(o_ref[...], w_ref[...])


def _oproj(x, o, w):
    n, d = x.shape
    tm = min(512, n)
    return pl.pallas_call(
        _oproj_body,
        grid=(n // tm,),
        in_specs=[pl.BlockSpec((tm, d), lambda i: (i, 0)),
                  pl.BlockSpec((tm, Q_DIM), lambda i: (i, 0)),
                  pl.BlockSpec((Q_DIM, d), lambda i: (0, 0))],
        out_specs=pl.BlockSpec((tm, d), lambda i: (i, 0)),
        out_shape=jax.ShapeDtypeStruct((n, d), F32),
        compiler_params=_params(("parallel",)),
        name="oproj",
    )(x, o, w)


def _nsa_proj_body(x_ref, g_ref, wq_ref, wkv_ref, wgt_ref, c_ref, s1_ref, s2_ref,
                   q_ref, qr_ref, gt_ref, cmp_ref, slc_ref, win_ref, slcb_ref, winb_ref):
    h = _rms(x_ref[...], g_ref[...]).astype(BF16)
    c, s1, s2 = c_ref[...], s1_ref[...], s2_ref[...]
    q = _dot(h, wq_ref[...])
    q_ref[...] = (q * Q_SCALE).astype(BF16)
    qr_ref[...] = (_rope(q, c, s1, s2) * Q_SCALE).astype(BF16)
    gt_ref[...] = jax.nn.sigmoid(_dot(h, wgt_ref[...]))
    kv = _dot(h, wkv_ref[...])
    cmp_ref[...] = kv[:, :KV_COLS]
    for j, (f_ref, b_ref) in enumerate(((slc_ref, slcb_ref), (win_ref, winb_ref))):
        base = (j + 1) * KV_COLS
        kr = _rope(kv[:, base:base + KV_DIM], c, s1, s2)
        row = jnp.concatenate([kr, kv[:, base + KV_DIM:base + KV_COLS]], axis=1)
        f_ref[...] = row
        b_ref[...] = row.astype(BF16)


def _nsa_proj(x, g, wq, wkv, wgt, tabs, n_tab_blocks):
    n, d = x.shape
    tm = min(512, n)
    row = lambda w: pl.BlockSpec((tm, w), lambda i: (i, 0))
    full = lambda a: pl.BlockSpec(a.shape, lambda i: (0, 0))
    tab = pl.BlockSpec((tm, LANES), lambda i: (i % n_tab_blocks, 0))
    outs = [(Q_DIM, BF16), (Q_DIM, BF16), (2 * LANES, F32), (KV_COLS, F32), (KV_COLS, F32),
            (KV_COLS, F32), (KV_COLS, BF16), (KV_COLS, BF16)]
    return pl.pallas_call(
        _nsa_proj_body,
        grid=(n // tm,),
        in_specs=[row(d), pl.BlockSpec((1, d), lambda i: (0, 0)), full(wq), full(wkv), full(wgt),
                  tab, tab, tab],
        out_specs=[row(w) for w, _ in outs],
        out_shape=[jax.ShapeDtypeStruct((n, w), t) for w, t in outs],
        compiler_params=_params(("parallel",)),
        name="nsa_proj",
    )(x, g.reshape(1, d), wq, wkv, wgt, *tabs)


def _dsa_proj_body(x_ref, g_ref, wq_ref, wkv_ref, wqi_ref, wki_ref, wwi_ref, c_ref, s1_ref, s2_ref,
                   q_ref, kv_ref, kvb_ref, qi_ref, ki_ref, kib_ref, wi_ref):
    h = _rms(x_ref[...], g_ref[...]).astype(BF16)
    c, s1, s2 = c_ref[...], s1_ref[...], s2_ref[...]
    q_ref[...] = (_rope(_dot(h, wq_ref[...]), c, s1, s2) * Q_SCALE).astype(BF16)
    kv = _dot(h, wkv_ref[...])
    row = jnp.concatenate([_rope(kv[:, :KV_DIM], c, s1, s2), kv[:, KV_DIM:]], axis=1)
    kv_ref[...] = row
    kvb_ref[...] = row.astype(BF16)
    qi_ref[...] = (_rope(_dot(h, wqi_ref[...]), c, s1, s2) * IDX_SCALE).astype(BF16)
    ki = _rope(_dot(h, wki_ref[...]), c, s1, s2)[:, :IDX_DIM]
    ki_ref[...] = ki
    kib_ref[...] = ki.astype(BF16)
    wi_ref[...] = _dot(h, wwi_ref[...]) * N_IDX ** -0.5


def _dsa_proj(x, g, wq, wkv, wqi, wki, wwi, tabs, n_tab_blocks):
    n, d = x.shape
    tm = min(512, n)
    row = lambda w: pl.BlockSpec((tm, w), lambda i: (i, 0))
    full = lambda a: pl.BlockSpec(a.shape, lambda i: (0, 0))
    tab = pl.BlockSpec((tm, LANES), lambda i: (i % n_tab_blocks, 0))
    outs = [(Q_DIM, BF16), (KV_COLS, F32), (KV_COLS, BF16), (N_IDX * LANES, BF16),
            (IDX_DIM, F32), (IDX_DIM, BF16), (LANES, F32)]
    return pl.pallas_call(
        _dsa_proj_body,
        grid=(n // tm,),
        in_specs=[row(d), pl.BlockSpec((1, d), lambda i: (0, 0)), full(wq), full(wkv), full(wqi),
                  full(wki), full(wwi), tab, tab, tab],
        out_specs=[row(w) for w, _ in outs],
        out_shape=[jax.ShapeDtypeStruct((n, w), t) for w, t in outs],
        compiler_params=_params(("parallel",)),
        name="dsa_proj",
    )(x, g.reshape(1, d), wq, wkv, wqi, wki, wwi, *tabs)


def _compress_body(*refs, n_in, nblk, paged):
    if paged:
        refs = refs[1:]
    x_refs = refs[:n_in]
    pos_ref, w1_ref, w2_ref, o_ref = refs[n_in:n_in + 4]
    x_scr = refs[n_in + 4]
    rows = x_refs[0].shape[1]
    for p in range(n_in):
        for j in range(KV_COLS // LANES):
            x_scr[j, p * rows:(p + 1) * rows, :] = x_refs[p][0, :, j * LANES:(j + 1) * LANES]
    load = lambda l, col: x_scr[col // LANES, pl.ds(l, nblk, stride=BLOCK), :]
    outs = []
    for c in range(2):
        acc = jnp.zeros((N_KV * nblk, w1_ref.shape[-1]), F32)
        for l in range(BLOCK):
            pieces = []
            for kp in range(N_KV // 2):
                col = c * KV_DIM + kp * LANES
                a = (load(l, col) + pos_ref[l:l + 1, col:col + LANES]).astype(BF16)
                pieces += [a[:, :HEAD_DIM], a[:, HEAD_DIM:]]
            acc = acc + _dot(jnp.concatenate(pieces, axis=0), w1_ref[c, l])
        out = _dot(jax.nn.gelu(acc).astype(BF16), w2_ref[c])
        outs += [out[k * nblk:(k + 1) * nblk] for k in range(N_KV)]
    o_ref[...] = jnp.concatenate(outs, axis=1).astype(BF16)


def _compress_rows(x3, pos, w1, w2):
    g, r, _ = x3.shape
    nblk = r // BLOCK
    full = lambda a: pl.BlockSpec(a.shape, lambda i: (0,) * a.ndim)
    return pl.pallas_call(
        functools.partial(_compress_body, n_in=1, nblk=nblk, paged=False),
        grid=(g,),
        in_specs=[pl.BlockSpec((1, r, KV_COLS), lambda i: (i, 0, 0)), full(pos), full(w1), full(w2)],
        out_specs=pl.BlockSpec((nblk, KV_COLS), lambda i: (i, 0)),
        out_shape=jax.ShapeDtypeStruct((g * nblk, KV_COLS), BF16),
        scratch_shapes=[pltpu.VMEM((KV_COLS // LANES, r, LANES), F32)],
        compiler_params=_params(("parallel",)),
        name="compress_rows",
    )(x3, pos, w1, w2)


def _compress_paged_body(pt_ref, *refs, n_in):
    del pt_ref
    x_refs = refs[:n_in]
    pos_ref, w1_ref, w2_ref, o_ref, x_scr = refs[n_in:]
    per_page = N_KV * HEAD_DIM
    for p in range(n_in):
        for c in range(2):
            x_scr[c, p * per_page:(p + 1) * per_page, :] = x_refs[p][0, c].reshape(per_page, PAGE_SIZE)
    m = n_in * N_KV
    for c in range(2):
        acc = jnp.zeros((2 * m, w1_ref.shape[-1]), F32)
        for d in range(HEAD_DIM):
            a = (x_scr[c, pl.ds(d, m, stride=HEAD_DIM), :] + pos_ref[c, d:d + 1, :]).astype(BF16)
            acc = acc + _dot(jnp.concatenate([a[:, :BLOCK], a[:, BLOCK:]], axis=0), w1_ref[c, d])
        o_ref[0, c] = _dot(jax.nn.gelu(acc).astype(BF16), w2_ref[c]).astype(BF16)


def _compress_paged(pool, page_table, pos, w1, w2):
    db, n_pages = page_table.shape
    per_step = 2
    n_in = per_step * n_pages
    steps = db // per_step
    rows = 2 * n_in * N_KV
    full = lambda a: pl.BlockSpec(a.shape, lambda i, pt: (0,) * a.ndim)
    page = lambda p: pl.BlockSpec(
        (1, 2, N_KV, HEAD_DIM, PAGE_SIZE),
        lambda i, pt: (pt[per_step * i + p // n_pages, p % n_pages], 0, 0, 0, 0))
    out = pl.pallas_call(
        functools.partial(_compress_paged_body, n_in=n_in),
        grid_spec=pltpu.PrefetchScalarGridSpec(
            num_scalar_prefetch=1,
            grid=(steps,),
            in_specs=[page(p) for p in range(n_in)] + [full(pos), full(w1), full(w2)],
            out_specs=pl.BlockSpec((1, 2, rows, HEAD_DIM), lambda i, pt: (i, 0, 0, 0)),
            scratch_shapes=[pltpu.VMEM((2, n_in * N_KV * HEAD_DIM, PAGE_SIZE), F32)]),
        out_shape=jax.ShapeDtypeStruct((steps, 2, rows, HEAD_DIM), BF16),
        compiler_params=_params(("parallel",)),
        name="compress_paged",
    )(page_table, *([pool] * n_in), pos, w1, w2)
    out = out.reshape(steps, 2, 2, per_step, n_pages, N_KV, HEAD_DIM)
    return out.transpose(0, 3, 4, 2, 1, 5, 6).reshape(db * n_pages * 2, KV_COLS)


def _pair_rows(blk, half, tq):
    lane = _iota((1, LANES), 1)
    keep = (lane < HEAD_DIM) if half == 0 else (lane >= HEAD_DIM)
    zero = jnp.zeros((tq, LANES), blk.dtype)
    return jnp.concatenate(
        [jnp.where(keep, blk[:, g * LANES:(g + 1) * LANES], zero) for g in range(GROUP)], axis=0)


def _own_block():
    return jnp.right_shift(_iota((8, KV_DIM), 1), 6) == _iota((8, KV_DIM), 0)


def _head_rows(q_row):
    keep = _own_block()
    q32 = q_row.astype(F32)
    slabs = []
    for g in range(GROUP):
        qg = jnp.concatenate(
            [q32[:, (kp * GROUP + g) * LANES:(kp * GROUP + g + 1) * LANES] for kp in range(2)], axis=1)
        slabs.append(jnp.where(keep, jnp.broadcast_to(qg, (8, KV_DIM)), 0.0))
    return jnp.concatenate(slabs, axis=0).astype(q_row.dtype)


def _head_out(o_rows):
    keep = _own_block()
    per_g = [jnp.sum(jnp.where(keep, o_rows[g * 8:(g + 1) * 8], 0.0), axis=0, keepdims=True)
             for g in range(GROUP)]
    return jnp.concatenate(
        [per_g[g][:, kp * LANES:(kp + 1) * LANES] for kp in range(2) for g in range(GROUP)], axis=1)


def _select_top(score, n):
    idx = _iota(score.shape, 0).astype(F32)
    sel = jnp.zeros(score.shape, F32)
    for _ in range(n):
        m = jnp.max(score, axis=0, keepdims=True)
        j = jnp.min(jnp.where(score == m, idx, 1e9), axis=0, keepdims=True)
        hit = idx == j
        sel = jnp.where(hit, jnp.where(m >= 0.0, 1.0, 0.0), sel)
        score = jnp.where(hit, -2.0, score)
    return sel


def _softmax_masked_cols(s, mask):
    s = jnp.where(mask, s, NEG)
    p = jnp.where(mask, jnp.exp(s - jnp.max(s, axis=0, keepdims=True)), 0.0)
    return p / jnp.maximum(jnp.sum(p, axis=0, keepdims=True), 1e-30)


def _block_scores(p_t, qpos_l, n_q):
    nb = p_t.shape[0]
    imp = p_t[:, 0:n_q]
    for g in range(1, GROUP):
        imp = imp + p_t[:, g * n_q:(g + 1) * n_q]
    blk = _iota((nb, 1), 0)
    cur = jnp.right_shift(qpos_l, 6)
    forced = (blk == 0) | (blk == cur) | (blk == cur - 1)
    score = jnp.where(forced, FORCE, imp)
    return jnp.where(blk <= cur, score, -1.0)


def _dot_tn(a, b):
    return lax.dot_general(a, b, (((0,), (0,)), ((), ())), preferred_element_type=F32)


def _attend_cols(streams, n_chunks):
    def body(c, carry):
        out = []
        for (q_rows, get_k, get_v, get_bias), (m, l, acc) in zip(streams, carry):
            st = _dot_nt(get_k(c), q_rows) + jnp.concatenate([get_bias(c)] * GROUP, axis=1)
            m_new = jnp.maximum(m, jnp.max(st, axis=0, keepdims=True))
            alpha = jnp.exp(m - m_new)
            p = jnp.exp(st - m_new)
            l = alpha * l + jnp.sum(p, axis=0, keepdims=True)
            acc = alpha * acc + _dot_tn(get_v(c), p.astype(BF16))
            out.append((m_new, l, acc))
        return tuple(out)

    init = tuple((jnp.full((1, s[0].shape[0]), NEG, F32), jnp.zeros((1, s[0].shape[0]), F32),
                  jnp.zeros((LANES, s[0].shape[0]), F32)) for s in streams)
    final = lax.fori_loop(0, n_chunks, body, init)
    return [acc / jnp.maximum(l, 1e-30) for _, l, acc in final]


def _nsa_core(q_blk, qr_blk, gates, kc, vc, get_k, get_v, n_chunks, ck, wk, wv, wstart, qpos_l):
    tq = q_blk.shape[0]
    nb = kc.shape[0]
    wlen = wk.shape[0]
    sub = _iota((LANES, 1), 0)
    qpos_lr = jnp.concatenate([qpos_l] * GROUP, axis=1)
    blk = _iota((nb, 1), 0)
    gates_t = gates.T
    qr_rows, o_cmps, streams = [], [], []
    for half in range(2):
        q_rows = _pair_rows(q_blk, half, tq)
        qr_rows.append(_pair_rows(qr_blk, half, tq))
        cmask = ((blk + 1) * BLOCK - 1) <= qpos_lr
        p_t = _softmax_masked_cols(_dot_nt(kc, q_rows), cmask)
        o_cmps.append(_dot_tn(vc, p_t.astype(BF16)))
        sel_t = _select_top(_block_scores(p_t, qpos_l, tq), min(N_SEL, nb)).astype(BF16)

        def slc_bias(c, sel_t=sel_t):
            kpos = c * ck + _iota((ck, 1), 0)
            expand = jnp.where(jnp.right_shift(kpos, 6) == _iota((1, nb), 1), 1.0, 0.0).astype(BF16)
            chosen = _dot(expand, sel_t)
            return jnp.where((chosen > 0.5) & (kpos <= qpos_l), 0.0, NEG)

        streams.append((qr_rows[half], get_k, get_v, slc_bias))
    o_slcs = _attend_cols(streams, n_chunks)
    per_half = []
    for half in range(2):
        o_cmp, o_slc = o_cmps[half], o_slcs[half]
        kwpos = wstart + _iota((wlen, 1), 0)
        dpos = qpos_lr - kwpos
        wmask = (dpos >= 0) & (dpos <= WINDOW) & (kwpos >= 0)
        o_win = _dot_tn(wv, _softmax_masked_cols(_dot_nt(wk, qr_rows[half]), wmask).astype(BF16))
        outs = []
        for g in range(GROUP):
            c0 = half * GROUP * 3 + g * 3
            sl = slice(g * tq, (g + 1) * tq)
            outs.append(gates_t[c0:c0 + 1] * o_cmp[:, sl] + gates_t[c0 + 1:c0 + 2] * o_slc[:, sl]
                        + gates_t[c0 + 2:c0 + 3] * o_win[:, sl])
        per_half.append(outs)
    return jnp.concatenate(
        [jnp.where(sub < HEAD_DIM, per_half[0][g], per_half[1][g]).T for g in range(GROUP)], axis=1)


def _nsa_prompt_body(q_ref, qr_ref, gt_ref, kc_ref, vc_ref, sk_ref, sv_ref, wk_ref, wv_ref, o_ref,
                     *, tq, ck):
    s0 = pl.program_id(2) * tq
    n_chunks = (s0 + tq + ck - 1) // ck
    wstart = pl.multiple_of(jnp.maximum(s0 - WINDOW, 0), tq)
    chunk = lambda ref: (lambda c: ref[pl.ds(pl.multiple_of(c * ck, ck), ck), :])
    out = _nsa_core(q_ref[...], qr_ref[...], gt_ref[...], kc_ref[...], vc_ref[...],
                    chunk(sk_ref), chunk(sv_ref), n_chunks, ck,
                    wk_ref[pl.ds(wstart, WINDOW + tq), :], wv_ref[pl.ds(wstart, WINDOW + tq), :],
                    wstart, s0 + _iota((1, tq), 1))
    o_ref[...] = out.astype(BF16)


def _nsa_attn_prompt(q, qr, gates, kvc, slc_b, win_b, b, t):
    tq = 128
    ck = min(512, t)
    nt = t // tq
    nb = t // BLOCK
    qspec = pl.BlockSpec((tq, 4 * LANES), lambda bi, kp, i: (bi * nt + i, kp))
    seq = lambda off: pl.BlockSpec((t, LANES), lambda bi, kp, i: (bi, kp + off))
    return pl.pallas_call(
        functools.partial(_nsa_prompt_body, tq=tq, ck=ck),
        grid=(b, 2, nt),
        in_specs=[qspec, qspec,
                  pl.BlockSpec((tq, LANES), lambda bi, kp, i: (bi * nt + i, kp)),
                  pl.BlockSpec((nb, LANES), lambda bi, kp, i: (bi, kp)),
                  pl.BlockSpec((nb, LANES), lambda bi, kp, i: (bi, kp + 2)),
                  seq(0), seq(2), seq(0), seq(2)],
        out_specs=qspec,
        out_shape=jax.ShapeDtypeStruct((b * t, Q_DIM), BF16),
        compiler_params=_params(("parallel", "parallel", "arbitrary")),
        name="nsa_attn_prompt",
    )(q, qr, gates, kvc, kvc, slc_b, slc_b, win_b, win_b)


def _tail_rows(row, n, dtype):
    w = row.shape[-1]
    first = _iota((n, w), 0) == 0
    return jnp.where(first, jnp.broadcast_to(row, (n, w)), 0.0).astype(dtype)


def _tail_cols(row, n):
    return _tail_rows(row, n, F32).T.reshape(2, N_KV, HEAD_DIM, n).astype(BF16)


def _feature_major(a):
    return jnp.moveaxis(a, -4, -1)


def _nsa_sample_body(pt_ref, q_ref, qr_ref, gt_ref, kcp_ref, kcn_ref, *rest,
                     n_pages, n_past, wb, tq, tail):
    del pt_ref
    pages = rest[:n_pages]
    slcn_ref, wst_ref, winn_ref, o_ref, kc_scr, slc_scr, win_scr = rest[n_pages:]
    nbp = kcp_ref.shape[0]
    kc_scr[...] = jnp.zeros_like(kc_scr)
    kc_scr[0:nbp, :] = kcp_ref[...]
    kc_scr[nbp:nbp + 16, :] = _tail_rows(kcn_ref[0].astype(F32), 16, BF16)
    for p in range(n_pages):
        slc_scr[:, :, :, p * PAGE_SIZE:(p + 1) * PAGE_SIZE] = pages[p][0].astype(BF16)
    slc_scr[:, :, :, n_past:n_past + tail] = _tail_cols(slcn_ref[0], tail)
    win_scr[:, :, :, 0:wb] = wst_ref[0].astype(BF16)
    win_scr[:, :, :, wb:wb + tail] = _tail_cols(winn_ref[0], tail)
    nb = kc_scr.shape[0]
    n_keys = n_past + tail
    q32, qr32 = _head_rows(q_ref[0]), _head_rows(qr_ref[0])
    blk = _iota((nb, 1), 0)
    gap = jnp.zeros((LANES - 8, KV_DIM), BF16)
    q_cols = jnp.concatenate(
        [x for g in range(GROUP) for x in (q32[g * 8:(g + 1) * 8], gap)], axis=0)
    cmask = ((blk + 1) * BLOCK - 1) <= n_past
    p_t = _softmax_masked_cols(_dot_nt(kc_scr[:, 0:KV_DIM], q_cols), cmask)
    p_rows = p_t.T
    p32 = jnp.concatenate([p_rows[g * LANES:g * LANES + 8] for g in range(GROUP)], axis=0)
    o_cmp = _dot(p32.astype(BF16), kc_scr[:, KV_DIM:KV_COLS])
    qpos_l = jnp.full((1, LANES), n_past, I32)
    sel8 = _select_top(_block_scores(p_t, qpos_l, LANES), min(N_SEL, nb)).T[0:8]
    sel32 = _tile_rows(sel8, GROUP).astype(BF16)
    kpos = _iota((1, n_keys), 1)
    expand = jnp.where(jnp.right_shift(kpos, 6) == blk, 1.0, 0.0).astype(BF16)
    smask = (_dot(sel32, expand) > 0.5) & (kpos <= n_past)
    p = _softmax_masked(_dot(qr32, slc_scr[0].reshape(KV_DIM, n_keys)), smask)
    o_slc = _dot_nt(p.astype(BF16), slc_scr[1].reshape(KV_DIM, n_keys))
    wlen = wb + tail
    kwpos = (n_past - wb) + _iota((1, wlen), 1)
    dpos = n_past - kwpos
    wmask = (dpos >= 0) & (dpos <= WINDOW) & (kwpos >= 0)
    p = _softmax_masked(_dot(qr32, win_scr[0].reshape(KV_DIM, wlen)), wmask)
    o_win = _dot_nt(p.astype(BF16), win_scr[1].reshape(KV_DIM, wlen))
    rows = q32.shape[0]
    r = _iota((rows, 2 * LANES), 0)
    k, g = r & 7, jnp.right_shift(r, 3)
    base = jnp.right_shift(k, 1) * LANES + (k & 1) * (GROUP * 3) + g * 3
    col = _iota((rows, 2 * LANES), 1)
    gb = jnp.broadcast_to(gt_ref[0], (rows, 2 * LANES))
    gate = lambda br: jnp.sum(jnp.where(col == base + br, gb, 0.0), axis=-1, keepdims=True)
    o_ref[0] = _head_out(gate(0) * o_cmp + gate(1) * o_slc + gate(2) * o_win).astype(BF16)


def _nsa_attn_sample(q, qr, gates, kvc_past, kvc_new, cache_slc, page_table, slc_new, win_state,
                     win_new):
    db, n_pages = page_table.shape
    n_past = n_pages * PAGE_SIZE
    wb = win_state.shape[-1]
    nbp = n_past // BLOCK
    tq, tail = 8, PAGE_SIZE
    nb = 64
    assert nb * BLOCK >= n_past + tail and nbp + 16 <= nb
    one = lambda w: pl.BlockSpec((1, 1, w), lambda i, pt: (i, 0, 0))
    fm = lambda n: (1, 2, N_KV, HEAD_DIM, n)
    page = lambda p: pl.BlockSpec(fm(PAGE_SIZE), lambda i, pt: (pt[i, p], 0, 0, 0, 0))
    r3 = lambda a: a.reshape(db, 1, a.shape[-1])
    return pl.pallas_call(
        functools.partial(_nsa_sample_body, n_pages=n_pages, n_past=n_past, wb=wb, tq=tq, tail=tail),
        grid_spec=pltpu.PrefetchScalarGridSpec(
            num_scalar_prefetch=1,
            grid=(db,),
            in_specs=[one(Q_DIM), one(Q_DIM), one(2 * LANES),
                      pl.BlockSpec((nbp, KV_COLS), lambda i, pt: (i, 0)), one(KV_COLS)]
                     + [page(p) for p in range(n_pages)]
                     + [one(KV_COLS), pl.BlockSpec(fm(wb), lambda i, pt: (i, 0, 0, 0, 0)),
                        one(KV_COLS)],
            out_specs=one(Q_DIM),
            scratch_shapes=[pltpu.VMEM((nb, KV_COLS), BF16),
                            pltpu.VMEM(fm(n_past + tail)[1:], BF16),
                            pltpu.VMEM(fm(wb + tail)[1:], BF16)]),
        out_shape=jax.ShapeDtypeStruct((db, 1, Q_DIM), BF16),
        compiler_params=_params(("parallel",)),
        name="nsa_attn_sample",
    )(page_table, r3(q), r3(qr), r3(gates), kvc_past, r3(kvc_new), *([cache_slc] * n_pages),
      r3(slc_new), win_state, r3(win_new)).reshape(db, Q_DIM)


def _dsa_select(qi_blk, wi, get_ki, n_chunks, ck, qpos_t, key_scr, topk, kt, same_rows=False):
    tq = qi_blk.shape[0]
    qi_rows = jnp.concatenate(
        [qi_blk[:, h * LANES:h * LANES + IDX_DIM] for h in range(N_IDX)], axis=0)
    wcol = [wi[:, h:h + 1] for h in range(N_IDX)]

    def loop(fn, init):
        if isinstance(n_chunks, int):
            for c in range(n_chunks):
                init = fn(c, init)
            return init
        return lax.fori_loop(0, n_chunks, fn, init)

    def kpos_of(c):
        return c * ck + _iota((1, ck), 1)

    def score_chunk(c, carry):
        sc = _dot(qi_rows, get_ki(c)) if kt else _dot_nt(qi_rows, get_ki(c))
        score = jnp.maximum(sc[0:tq], 0.0) * wcol[0]
        for h in range(1, N_IDX):
            score = score + jnp.maximum(sc[h * tq:(h + 1) * tq], 0.0) * wcol[h]
        score = jnp.where(kpos_of(c) <= qpos_t, score, NEG)
        score = jnp.where(score == 0.0, 0.0, score)
        bits = lax.bitcast_convert_type(score, I32)
        key_scr[c] = jnp.where(bits < 0, bits ^ 0x7FFFFFFF, bits)
        return carry

    loop(score_chunk, 0)

    def count(pred):
        def fn(c, acc):
            hit = jnp.where(pred(key_scr[c], kpos_of(c)), 1.0, 0.0)
            part = hit[:, 0:LANES]
            for j in range(1, ck // LANES):
                part = part + hit[:, j * LANES:(j + 1) * LANES]
            return acc + part
        return jnp.sum(loop(fn, jnp.zeros((tq, LANES), F32)), axis=-1, keepdims=True)

    kf = float(topk)
    if same_rows:
        assert tq == 8
        row = _iota((tq, 1), 0)
        digit = jnp.minimum(row + 1, 7)

        def digit_step(it, lo):
            shift = jnp.maximum(29 - 3 * it, 0)
            cand = lo + jnp.left_shift(digit, shift)
            ok = (count(lambda k, p: k >= cand) >= kf) & (row < 7)
            kept = jnp.sum(jnp.where(ok, 1.0, 0.0), axis=0, keepdims=True).astype(I32)
            return lo + jnp.left_shift(kept, shift)

        thr = lax.fori_loop(0, 11, digit_step, jnp.full((tq, 1), INT_MIN, I32))
    else:
        lo = jnp.where(count(lambda k, p: k >= 0) >= kf, 0, INT_MIN).astype(I32)

        def bit_step(it, lo):
            cand = lo + jnp.left_shift(jnp.int32(1), 30 - it)
            return jnp.where(count(lambda k, p: k >= cand) >= kf, cand, lo)

        thr = lax.fori_loop(0, 31, bit_step, lo)
    need = kf - count(lambda k, p: k > thr)
    surplus = jnp.max(count(lambda k, p: k >= thr)) > kf

    def idx_step(it, lo):
        mid = lo + jnp.left_shift(jnp.int32(1), 13 - it)
        enough = count(lambda k, p: (k == thr) & (p <= mid)) >= need
        return jnp.where(enough, lo, mid)

    jmax = lax.cond(
        surplus,
        lambda: lax.fori_loop(0, 14, idx_step, jnp.full((tq, 1), -1, I32)) + 1,
        lambda: jnp.full((tq, 1), 2 ** 14, I32))

    def bias_chunk(c, carry):
        k = key_scr[c]
        p = kpos_of(c)
        keep = ((k > thr) | ((k == thr) & (p <= jmax))) & (p <= qpos_t)
        key_scr[c] = lax.bitcast_convert_type(jnp.where(keep, 0.0, NEG), I32)
        return carry

    loop(bias_chunk, 0)


def _dsa_select_cols(qi_blk, wi, get_ki, n_chunks, ck, qpos_l, key_scr, topk):
    tq = qi_blk.shape[0]
    qi_rows = jnp.concatenate(
        [qi_blk[:, h * LANES:h * LANES + IDX_DIM] for h in range(N_IDX)], axis=0)
    wi_t = wi.T

    def loop(fn, init):
        return lax.fori_loop(0, n_chunks, fn, init)

    def kpos_of(c):
        return c * ck + _iota((ck, 1), 0)

    def score_chunk(c, carry):
        sc = _dot_nt(get_ki(c), qi_rows)
        score = jnp.maximum(sc[:, 0:tq], 0.0) * wi_t[0:1]
        for h in range(1, N_IDX):
            score = score + jnp.maximum(sc[:, h * tq:(h + 1) * tq], 0.0) * wi_t[h:h + 1]
        score = jnp.where(kpos_of(c) <= qpos_l, score, NEG)
        score = jnp.where(score == 0.0, 0.0, score)
        bits = lax.bitcast_convert_type(score, I32)
        key_scr[c] = jnp.where(bits < 0, bits ^ 0x7FFFFFFF, bits)
        return carry

    loop(score_chunk, 0)

    def count(pred):
        def fn(c, acc):
            hit = jnp.where(pred(key_scr[c], kpos_of(c)), 1.0, 0.0)
            part = jnp.sum(hit.reshape(8, ck // 64, 8, tq), axis=1)
            return acc + jnp.sum(part, axis=0)
        return jnp.sum(loop(fn, jnp.zeros((8, tq), F32)), axis=0, keepdims=True)

    kf = float(topk)
    lo = jnp.where(count(lambda k, p: k >= 0) >= kf, 0, INT_MIN).astype(I32)

    def bit_step(it, lo):
        cand = lo + jnp.left_shift(jnp.int32(1), 30 - it)
        return jnp.where(count(lambda k, p: k >= cand) >= kf, cand, lo)

    thr = lax.fori_loop(0, 31, bit_step, lo)
    need = kf - count(lambda k, p: k > thr)
    surplus = jnp.max(count(lambda k, p: k >= thr)) > kf

    def idx_step(it, lo):
        mid = lo + jnp.left_shift(jnp.int32(1), 13 - it)
        enough = count(lambda k, p: (k == thr) & (p <= mid)) >= need
        return jnp.where(enough, lo, mid)

    jmax = lax.cond(
        surplus,
        lambda: lax.fori_loop(0, 14, idx_step, jnp.full((1, tq), -1, I32)) + 1,
        lambda: jnp.full((1, tq), 2 ** 14, I32))

    def bias_chunk(c, carry):
        k = key_scr[c]
        p = kpos_of(c)
        keep = ((k > thr) | ((k == thr) & (p <= jmax))) & (p <= qpos_l)
        key_scr[c] = lax.bitcast_convert_type(jnp.where(keep, 0.0, NEG), I32)
        return carry

    loop(bias_chunk, 0)


def _dsa_prompt_body(q_ref, qi_ref, wi_ref, ki_ref, k0_ref, k1_ref, v0_ref, v1_ref, o_ref, key_scr,
                     *, tq, ck, topk):
    s0 = pl.program_id(1) * tq
    qpos_t = s0 + _iota((tq, 1), 0)
    n_chunks = (s0 + tq + ck - 1) // ck
    rows = lambda c: pl.ds(pl.multiple_of(c * ck, ck), ck)
    k_refs, v_refs = (k0_ref, k1_ref), (v0_ref, v1_ref)
    del qpos_t
    _dsa_select_cols(qi_ref[...], wi_ref[...], lambda c: ki_ref[rows(c), :], n_chunks, ck,
                     s0 + _iota((1, tq), 1), key_scr, topk)
    get_bias = lambda c: lax.bitcast_convert_type(key_scr[c], F32)
    q_blk = q_ref[...]
    sub = _iota((LANES, 1), 0)
    streams = [(_pair_rows(q_blk[:, kp * 4 * LANES:(kp + 1) * 4 * LANES], half, tq),
                lambda c, kp=kp: k_refs[kp][rows(c), :], lambda c, kp=kp: v_refs[kp][rows(c), :],
                get_bias) for kp in range(2) for half in range(2)]
    outs = _attend_cols(streams, n_chunks)
    for kp in range(2):
        for g in range(GROUP):
            sl = slice(g * tq, (g + 1) * tq)
            cols = slice((kp * GROUP + g) * LANES, (kp * GROUP + g + 1) * LANES)
            o_ref[:, cols] = jnp.where(sub < HEAD_DIM, outs[2 * kp][:, sl],
                                       outs[2 * kp + 1][:, sl]).T.astype(BF16)


def _dsa_attn_prompt(q, qi, wi, ki_b, kv_b, b, t):
    tq = 128
    ck = min(512, t)
    nt = t // tq
    row = lambda w: pl.BlockSpec((tq, w), lambda bi, i: (bi * nt + i, 0))
    seq = lambda j: pl.BlockSpec((t, LANES), lambda bi, i: (bi, j))
    return pl.pallas_call(
        functools.partial(_dsa_prompt_body, tq=tq, ck=ck, topk=min(DSA_TOPK, t // 4)),
        grid=(b, nt),
        in_specs=[row(Q_DIM), row(N_IDX * LANES), row(LANES),
                  pl.BlockSpec((t, IDX_DIM), lambda bi, i: (bi, 0)),
                  seq(0), seq(1), seq(2), seq(3)],
        out_specs=row(Q_DIM),
        out_shape=jax.ShapeDtypeStruct((b * t, Q_DIM), BF16),
        scratch_shapes=[pltpu.VMEM((t // ck, ck, tq), I32)],
        compiler_params=_params(("parallel", "arbitrary")),
        name="dsa_attn_prompt",
    )(q, qi, wi, ki_b, kv_b, kv_b, kv_b, kv_b)


def _dsa_sample_body(pt_ref, q_ref, qi_ref, wi_ref, *rest, n_pages, n_past, tq, tail, topk):
    del pt_ref
    kv_pages = rest[:n_pages]
    ki_pages = rest[n_pages:2 * n_pages]
    kvn_ref, kin_ref, o_ref, kv_scr, ki_scr, key_scr = rest[2 * n_pages:]
    for p in range(n_pages):
        cols = slice(p * PAGE_SIZE, (p + 1) * PAGE_SIZE)
        kv_scr[:, :, :, cols] = kv_pages[p][0].astype(BF16)
        ki_scr[:, cols] = ki_pages[p][0].astype(BF16)
    kv_scr[:, :, :, n_past:n_past + tail] = _tail_cols(kvn_ref[0], tail)
    ki_scr[:, n_past:n_past + tail] = _tail_rows(kin_ref[0], tail, F32).T[0:IDX_DIM].astype(BF16)
    qpos_t = jnp.full((tq, 1), n_past, I32)
    ck = n_past + tail
    _dsa_select(jnp.broadcast_to(qi_ref[0], (tq, N_IDX * LANES)), jnp.broadcast_to(wi_ref[0], (tq, LANES)),
                lambda c: ki_scr[...], 1, ck, qpos_t, key_scr, topk, True, same_rows=True)
    mask = lax.bitcast_convert_type(key_scr[0, 0:1, :], F32) == 0.0
    p = _softmax_masked(_dot(_head_rows(q_ref[0]), kv_scr[0].reshape(KV_DIM, ck)), mask)
    o_ref[0] = _head_out(_dot_nt(p.astype(BF16), kv_scr[1].reshape(KV_DIM, ck))).astype(BF16)


def _dsa_attn_sample(q, qi, wi, cache_kv, cache_idx, page_table, kv_new, ki_new):
    db, n_pages = page_table.shape
    n_past = n_pages * PAGE_SIZE
    tq, tail = 8, PAGE_SIZE
    one = lambda w: pl.BlockSpec((1, 1, w), lambda i, pt: (i, 0, 0))
    kv_page = lambda p: pl.BlockSpec((1, 2, N_KV, HEAD_DIM, PAGE_SIZE),
                                     lambda i, pt: (pt[i, p], 0, 0, 0, 0))
    ki_page = lambda p: pl.BlockSpec((1, IDX_DIM, PAGE_SIZE), lambda i, pt: (pt[i, p], 0, 0))
    r3 = lambda a: a.reshape(db, 1, a.shape[-1])
    return pl.pallas_call(
        functools.partial(_dsa_sample_body, n_pages=n_pages, n_past=n_past, tq=tq, tail=tail,
                          topk=min(DSA_TOPK, (n_past + 1) // 4)),
        grid_spec=pltpu.PrefetchScalarGridSpec(
            num_scalar_prefetch=1,
            grid=(db,),
            in_specs=[one(Q_DIM), one(N_IDX * LANES), one(LANES)]
                     + [kv_page(p) for p in range(n_pages)]
                     + [ki_page(p) for p in range(n_pages)]
                     + [one(KV_COLS), one(LANES)],
            out_specs=one(Q_DIM),
            scratch_shapes=[pltpu.VMEM((2, N_KV, HEAD_DIM, n_past + tail), BF16),
                            pltpu.VMEM((IDX_DIM, n_past + tail), BF16),
                            pltpu.VMEM((1, tq, n_past + tail), I32)]),
        out_shape=jax.ShapeDtypeStruct((db, 1, Q_DIM), BF16),
        compiler_params=_params(("parallel",)),
        name="dsa_attn_sample",
    )(page_table, r3(q), r3(qi), r3(wi), *([cache_kv] * n_pages), *([cache_idx] * n_pages),
      r3(kv_new), r3(ki_new)).reshape(db, Q_DIM)


def _rope_tables(pos):
    half = ROT_DIM // 2
    n = pos.shape[0]
    inv = ROPE_THETA ** (-jnp.arange(half, dtype=F32) * 2.0 / ROT_DIM)
    ang = pos.astype(F32)[:, None] * inv[None, :]
    cos, sin = jnp.cos(ang), jnp.sin(ang)
    z8 = jnp.zeros((n, half), F32)
    rest = HEAD_DIM - ROT_DIM
    c = jnp.concatenate([cos, cos, jnp.ones((n, rest), F32)], axis=1)
    s1 = jnp.concatenate([-sin, z8, jnp.zeros((n, rest), F32)], axis=1)
    s2 = jnp.concatenate([z8, sin, jnp.zeros((n, rest), F32)], axis=1)
    rep = LANES // HEAD_DIM
    return tuple(jnp.tile(a, (1, rep)) for a in (c, s1, s2))


def _pair_perm():
    cols = []
    for kp in range(N_KV // 2):
        for g in range(GROUP):
            for half in range(2):
                h = (2 * kp + half) * GROUP + g
                cols.extend(range(h * HEAD_DIM, (h + 1) * HEAD_DIM))
    return jnp.array(cols, dtype=I32)


def _pad_cols(w, n):
    return jnp.pad(w, ((0, 0), (0, n - w.shape[1])))


def _nsa_layer(xp, xs, g, w_in, w_out, cmp_pos, cmp_w1, cmp_w2, cache_cmp, cache_slc, win_state,
               page_table, tabs_p, tabs_s, b, t):
    db = xs.shape[0]
    perm = _pair_perm()
    wq = w_in[:, :Q_DIM][:, perm].astype(BF16)
    wkv = w_in[:, Q_DIM:Q_DIM + 3 * KV_COLS].astype(BF16)
    wg = w_in[:, Q_DIM + 3 * KV_COLS:]
    wgt = jnp.concatenate([_pad_cols(wg[:, kp * 2 * GROUP * 3:(kp + 1) * 2 * GROUP * 3], LANES)
                           for kp in range(N_KV // 2)], axis=1).astype(BF16)
    wo = w_out[perm, :].astype(BF16)
    pos = jnp.concatenate([jnp.tile(cmp_pos[:, c, :], (1, N_KV)) for c in range(2)], axis=1)
    w1 = cmp_w1.astype(BF16)
    w2 = cmp_w2.astype(BF16)

    q, qr, gates, kcmp, kslc, kwin, slc_b, win_b = _nsa_proj(xp, g, wq, wkv, wgt, tabs_p, t // min(512, t))
    kvc = _compress_rows(kcmp.reshape(b, t, KV_COLS), pos, w1, w2)
    o = _nsa_attn_prompt(q, qr, gates, kvc, slc_b, win_b, b, t)
    xp = _oproj(xp, o, wo)
    qs, qrs, gs, scmp, sslc, swin, _, _ = _nsa_proj(xs, g, wq, wkv, wgt, tabs_s, 1)
    pos_fm = jnp.tile(cmp_pos.transpose(1, 2, 0), (1, 1, PAGE_SIZE // BLOCK))
    w1_fm = cmp_w1.transpose(0, 2, 1, 3).astype(BF16)
    kvc_past = _compress_paged(_feature_major(cache_cmp), page_table, pos_fm, w1_fm, w2)
    new_rows = jnp.pad(scmp[:, None, :], ((0, 0), (0, BLOCK - 1), (0, 0)))
    kvc_new = _compress_rows(new_rows.reshape(2, db // 2 * BLOCK, KV_COLS), pos, w1, w2)
    os_ = _nsa_attn_sample(qs, qrs, gs, kvc_past, kvc_new, _feature_major(cache_slc), page_table,
                           sslc, _feature_major(win_state), swin)
    xs = _oproj(xs, os_, wo)
    wb = min(WINDOW, t)
    kv5 = lambda a, n: a.reshape(n, -1, 2, N_KV, HEAD_DIM)
    win_s = jnp.concatenate([win_state, kv5(swin, db)], axis=1)[:, 1:]
    outs = (kv5(kcmp, b), kv5(scmp, db), kv5(kslc, b), kv5(sslc, db),
            kv5(kwin, b)[:, t - wb:], win_s)
    return xp, xs, outs


def _dsa_layer(xp, xs, g, w_in, w_out, cache_kv, cache_idx, page_table, tabs_p, tabs_s, b, t):
    db = xs.shape[0]
    perm = _pair_perm()
    wq = w_in[:, :Q_DIM][:, perm].astype(BF16)
    o1 = Q_DIM + KV_COLS
    wkv = w_in[:, Q_DIM:o1].astype(BF16)
    o2 = o1 + N_IDX * IDX_DIM
    wqi = jnp.concatenate([_pad_cols(w_in[:, o1 + h * IDX_DIM:o1 + (h + 1) * IDX_DIM], LANES)
                           for h in range(N_IDX)], axis=1).astype(BF16)
    wki = _pad_cols(w_in[:, o2:o2 + IDX_DIM], LANES).astype(BF16)
    wwi = _pad_cols(w_in[:, o2 + IDX_DIM:], LANES).astype(BF16)
    wo = w_out[perm, :].astype(BF16)

    q, kv, kv_b, qi, ki, ki_b, wi = _dsa_proj(xp, g, wq, wkv, wqi, wki, wwi, tabs_p, t // min(512, t))
    o = _dsa_attn_prompt(q, qi, wi, ki_b, kv_b, b, t)
    xp = _oproj(xp, o, wo)
    qs, kvs, _, qis, kis, _, wis = _dsa_proj(xs, g, wq, wkv, wqi, wki, wwi, tabs_s, 1)
    os_ = _dsa_attn_sample(qs, qis, wis, _feature_major(cache_kv), jnp.swapaxes(cache_idx, -1, -2),
                           page_table, kvs, _pad_cols(kis, LANES))
    xs = _oproj(xs, os_, wo)
    outs = (kv.reshape(b, t, 2, N_KV, HEAD_DIM), kvs.reshape(db, 1, 2, N_KV, HEAD_DIM),
            ki.reshape(b, t, IDX_DIM), kis.reshape(db, 1, IDX_DIM))
    return xp, xs, outs


def kernel(x_prompt, x_sample, cache_nsa_cmp, cache_nsa_slc, state_nsa_win, cache_dsa_kv, cache_dsa_idx,
           page_table, norm_g, w_ffn_in, w_ffn_out, w_nsa_in, w_nsa_out, nsa_cmp_pos, nsa_cmp_w1,
           nsa_cmp_w2, w_dsa_in, w_dsa_out, final_norm_g):
    b, t, d = x_prompt.shape
    db, ds, _ = x_sample.shape
    assert ds == 1
    depth = norm_g.shape[0]
    n_past = page_table.shape[1] * PAGE_SIZE
    xp = x_prompt.reshape(b * t, d)
    xs = x_sample.reshape(db, d)
    tabs_p = _rope_tables(jnp.arange(t))
    tabs_s = _rope_tables(jnp.full((db,), n_past, I32))
    w_in_b = w_ffn_in.astype(BF16)
    w_out_b = w_ffn_out.astype(BF16)
    nsa_outs, dsa_outs = [], []
    for layer in range(depth):
        kind, j = layer % 2, layer // 2
        xp = _ffn(xp, norm_g[layer, 0], w_in_b[layer, 0], w_out_b[layer, 0])
        xs = _ffn(xs, norm_g[layer, 0], w_in_b[layer, 0], w_out_b[layer, 0])
        if kind == 0:
            xp, xs, outs = _nsa_layer(xp, xs, norm_g[layer, 1], w_nsa_in[j], w_nsa_out[j],
                                      nsa_cmp_pos[j], nsa_cmp_w1[j], nsa_cmp_w2[j], cache_nsa_cmp[j],
                                      cache_nsa_slc[j], state_nsa_win[j], page_table, tabs_p, tabs_s,
                                      b, t)
            nsa_outs.append(outs)
        else:
            xp, xs, outs = _dsa_layer(xp, xs, norm_g[layer, 1], w_dsa_in[j], w_dsa_out[j],
                                      cache_dsa_kv[j], cache_dsa_idx[j], page_table, tabs_p, tabs_s,
                                      b, t)
            dsa_outs.append(outs)
        fg = final_norm_g if layer == depth - 1 else None
        xp = _ffn(xp, norm_g[layer, 2], w_in_b[layer, 1], w_out_b[layer, 1], fg)
        xs = _ffn(xs, norm_g[layer, 2], w_in_b[layer, 1], w_out_b[layer, 1], fg)
    stack = lambda outs, k: jnp.stack([o[k] for o in outs])
    return (xp.reshape(b, t, d), xs.reshape(db, ds, d),
            stack(nsa_outs, 0), stack(nsa_outs, 1), stack(nsa_outs, 2), stack(nsa_outs, 3),
            stack(nsa_outs, 4), stack(nsa_outs, 5),
            stack(dsa_outs, 0), stack(dsa_outs, 1), stack(dsa_outs, 2), stack(dsa_outs, 3))
```

```python
import functools

import jax
import jax.numpy as jnp
from jax import lax
from jax.experimental import pallas as pl
from jax.experimental.pallas import tpu as pltpu

F32 = jnp.float32
BF16 = jnp.bfloat16
I32 = jnp.int32

D_MODEL = 1024
N_HEADS = 16
HEAD_DIM = 64
N_KV = 4
GROUP = N_HEADS // N_KV
Q_DIM = N_HEADS * HEAD_DIM
KV_DIM = N_KV * HEAD_DIM
ROT_DIM = HEAD_DIM // 4
ROPE_THETA = 500000.0
D_FF = 2816
BLOCK = 64
N_SEL = 8
WINDOW = 512
FORCE = 1.0e4
N_IDX = 8
IDX_DIM = 64
DSA_TOPK = 256
PAGE_SIZE = 128
RMS_EPS = 1e-6
NEG = -1e30

LANES = 128
KV_COLS = 2 * KV_DIM
LOG2E = 1.4426950408889634
Q_SCALE = HEAD_DIM ** -0.5 * LOG2E
IDX_SCALE = IDX_DIM ** -0.5
INT_MIN = -(2 ** 31)
VMEM_LIMIT = 56 * 2 ** 20


def _params(sem):
    return pltpu.CompilerParams(dimension_semantics=sem, vmem_limit_bytes=VMEM_LIMIT)


def _dot(a, b):
    return jnp.dot(a, b, preferred_element_type=F32)


def _dot_nt(a, b):
    return lax.dot_general(a, b, (((1,), (1,)), ((), ())), preferred_element_type=F32)


def _iota(shape, dim):
    return lax.broadcasted_iota(I32, shape, dim)


def _rms(x, g):
    return x * lax.rsqrt(jnp.mean(x * x, axis=-1, keepdims=True) + RMS_EPS) * g


def _rope(x, c, s1, s2):
    half = ROT_DIM // 2
    outs = []
    for j in range(x.shape[1] // LANES):
        xs = x[:, j * LANES:(j + 1) * LANES]
        outs.append(xs * c + pltpu.roll(xs, LANES - half, 1) * s1 + pltpu.roll(xs, half, 1) * s2)
    return outs[0] if len(outs) == 1 else jnp.concatenate(outs, axis=1)


def _softmax_masked(s, mask):
    s = jnp.where(mask, s, NEG)
    p = jnp.where(mask, jnp.exp2(s - jnp.max(s, axis=-1, keepdims=True)), 0.0)
    return p / jnp.maximum(jnp.sum(p, axis=-1, keepdims=True), 1e-30)


def _tile_rows(x, n):
    return jnp.concatenate([x] * n, axis=0)


def _ffn_body(*refs, n_ff, final):
    if final:
        x_ref, g_ref, wg_ref, wu_ref, wo_ref, fg_ref, o_ref, h_scr, acc_scr = refs
    else:
        x_ref, g_ref, wg_ref, wu_ref, wo_ref, o_ref, h_scr, acc_scr = refs
    j = pl.program_id(1)

    @pl.when(j == 0)
    def _():
        h_scr[...] = _rms(x_ref[...], g_ref[...]).astype(BF16)
        acc_scr[...] = jnp.zeros_like(acc_scr)

    h = h_scr[...]
    gate = _dot(h, wg_ref[...])
    up = _dot(h, wu_ref[...])
    act = (gate * jax.nn.sigmoid(gate) * up).astype(BF16)
    acc_scr[...] += _dot(act, wo_ref[...])

    @pl.when(j == n_ff - 1)
    def _():
        y = x_ref[...] + 0.5 * acc_scr[...]
        if final:
            y = _rms(y, fg_ref[...])
        o_ref[...] = y


def _ffn(x, g, w_in, w_out, final_g=None):
    n, d = x.shape
    tm = min(512, n)
    n_ff = 2
    fc = D_FF // n_ff
    in_specs = [
        pl.BlockSpec((tm, d), lambda i, j: (i, 0)),
        pl.BlockSpec((1, d), lambda i, j: (0, 0)),
        pl.BlockSpec((d, fc), lambda i, j: (0, j)),
        pl.BlockSpec((d, fc), lambda i, j: (0, j + n_ff)),
        pl.BlockSpec((fc, d), lambda i, j: (j, 0)),
    ]
    args = [x, g.reshape(1, d), w_in, w_in, w_out]
    if final_g is not None:
        in_specs.append(pl.BlockSpec((1, d), lambda i, j: (0, 0)))
        args.append(final_g.reshape(1, d))
    return pl.pallas_call(
        functools.partial(_ffn_body, n_ff=n_ff, final=final_g is not None),
        grid=(n // tm, n_ff),
        in_specs=in_specs,
        out_specs=pl.BlockSpec((tm, d), lambda i, j: (i, 0)),
        out_shape=jax.ShapeDtypeStruct((n, d), F32),
        scratch_shapes=[pltpu.VMEM((tm, d), BF16), pltpu.VMEM((tm, d), F32)],
        compiler_params=_params(("parallel", "arbitrary")),
        name="ffn",
    )(*args)


def _oproj_body(x_ref, o_ref, w_ref, y_ref):
    y_ref[...] = x_ref[...] + _dot(o_ref[...], w_ref[...])


def _oproj(x, o, w):
    n, d = x.shape
    tm = min(512, n)
    return pl.pallas_call(
        _oproj_body,
        grid=(n // tm,),
        in_specs=[pl.BlockSpec((tm, d), lambda i: (i, 0)),
                  pl.BlockSpec((tm, Q_DIM), lambda i: (i, 0)),
                  pl.BlockSpec((Q_DIM, d), lambda i: (0, 0))],
        out_specs=pl.BlockSpec((tm, d), lambda i: (i, 0)),
        out_shape=jax.ShapeDtypeStruct((n, d), F32),
        compiler_params=_params(("parallel",)),
        name="oproj",
    )(x, o, w)


def _nsa_proj_body(x_ref, g_ref, wq_ref, wkv_ref, wgt_ref, c_ref, s1_ref, s2_ref,
                   q_ref, qr_ref, gt_ref, cmp_ref, slc_ref, win_ref, slcb_ref, winb_ref,
                   cmpt_ref, slct_ref, wint_ref):
    h = _rms(x_ref[...], g_ref[...]).astype(BF16)
    c, s1, s2 = c_ref[...], s1_ref[...], s2_ref[...]
    q = _dot(h, wq_ref[...])
    q_ref[...] = (q * Q_SCALE).astype(BF16)
    qr_ref[...] = (_rope(q, c, s1, s2) * Q_SCALE).astype(BF16)
    gt_ref[...] = jax.nn.sigmoid(_dot(h, wgt_ref[...]))
    kv = _dot(h, wkv_ref[...])
    cmp_ref[...] = kv[:, :KV_COLS]
    cmpt_ref[0] = kv[:, :KV_COLS].T
    for j, (f_ref, b_ref, t_ref) in enumerate(((slc_ref, slcb_ref, slct_ref),
                                               (win_ref, winb_ref, wint_ref))):
        base = (j + 1) * KV_COLS
        kr = _rope(kv[:, base:base + KV_DIM], c, s1, s2)
        row = jnp.concatenate([kr, kv[:, base + KV_DIM:base + KV_COLS]], axis=1)
        f_ref[...] = row
        b_ref[...] = row.astype(BF16)
        t_ref[0] = row.T


def _nsa_proj(x, g, wq, wkv, wgt, tabs, n_tab_blocks, n_seq):
    n, d = x.shape
    tm = min(512, n)
    nt = n // n_seq // tm
    row = lambda w: pl.BlockSpec((tm, w), lambda i: (i, 0))
    full = lambda a: pl.BlockSpec(a.shape, lambda i: (0, 0))
    tab = pl.BlockSpec((tm, LANES), lambda i: (i % n_tab_blocks, 0))
    outs = [(Q_DIM, BF16), (Q_DIM, BF16), (2 * LANES, F32), (KV_COLS, F32), (KV_COLS, F32),
            (KV_COLS, F32), (KV_COLS, BF16), (KV_COLS, BF16)]
    fm = pl.BlockSpec((1, KV_COLS, tm), lambda i: (i // nt, 0, i % nt))
    return pl.pallas_call(
        _nsa_proj_body,
        grid=(n // tm,),
        in_specs=[row(d), pl.BlockSpec((1, d), lambda i: (0, 0)), full(wq), full(wkv), full(wgt),
                  tab, tab, tab],
        out_specs=[row(w) for w, _ in outs] + [fm] * 3,
        out_shape=[jax.ShapeDtypeStruct((n, w), t) for w, t in outs]
                  + [jax.ShapeDtypeStruct((n_seq, KV_COLS, n // n_seq), F32)] * 3,
        compiler_params=_params(("parallel",)),
        name="nsa_proj",
    )(x, g.reshape(1, d), wq, wkv, wgt, *tabs)


def _dsa_proj_body(x_ref, g_ref, wq_ref, wkv_ref, wqi_ref, wki_ref, wwi_ref, c_ref, s1_ref, s2_ref,
                   q_ref, kv_ref, kvb_ref, qi_ref, ki_ref, kib_ref, wi_ref, kvt_ref, kit_ref):
    h = _rms(x_ref[...], g_ref[...]).astype(BF16)
    c, s1, s2 = c_ref[...], s1_ref[...], s2_ref[...]
    q_ref[...] = (_rope(_dot(h, wq_ref[...]), c, s1, s2) * Q_SCALE).astype(BF16)
    kv = _dot(h, wkv_ref[...])
    row = jnp.concatenate([_rope(kv[:, :KV_DIM], c, s1, s2), kv[:, KV_DIM:]], axis=1)
    kv_ref[...] = row
    kvb_ref[...] = row.astype(BF16)
    kvt_ref[0] = row.T
    qi_ref[...] = (_rope(_dot(h, wqi_ref[...]), c, s1, s2) * IDX_SCALE).astype(BF16)
    ki_wide = _rope(_dot(h, wki_ref[...]), c, s1, s2)
    ki = ki_wide[:, :IDX_DIM]
    ki_ref[...] = ki
    kib_ref[...] = ki.astype(BF16)
    kit_ref[0] = ki_wide.T[0:IDX_DIM]
    wi_ref[...] = _dot(h, wwi_ref[...]) * N_IDX ** -0.5


def _dsa_proj(x, g, wq, wkv, wqi, wki, wwi, tabs, n_tab_blocks, n_seq):
    n, d = x.shape
    tm = min(512, n)
    nt = n // n_seq // tm
    row = lambda w: pl.BlockSpec((tm, w), lambda i: (i, 0))
    full = lambda a: pl.BlockSpec(a.shape, lambda i: (0, 0))
    tab = pl.BlockSpec((tm, LANES), lambda i: (i % n_tab_blocks, 0))
    outs = [(Q_DIM, BF16), (KV_COLS, F32), (KV_COLS, BF16), (N_IDX * LANES, BF16),
            (IDX_DIM, F32), (IDX_DIM, BF16), (LANES, F32)]
    fm = lambda w: pl.BlockSpec((1, w, tm), lambda i: (i // nt, 0, i % nt))
    return pl.pallas_call(
        _dsa_proj_body,
        grid=(n // tm,),
        in_specs=[row(d), pl.BlockSpec((1, d), lambda i: (0, 0)), full(wq), full(wkv), full(wqi),
                  full(wki), full(wwi), tab, tab, tab],
        out_specs=[row(w) for w, _ in outs] + [fm(KV_COLS), fm(IDX_DIM)],
        out_shape=[jax.ShapeDtypeStruct((n, w), t) for w, t in outs]
                  + [jax.ShapeDtypeStruct((n_seq, KV_COLS, n // n_seq), F32),
                     jax.ShapeDtypeStruct((n_seq, IDX_DIM, n // n_seq), F32)],
        compiler_params=_params(("parallel",)),
        name="dsa_proj",
    )(x, g.reshape(1, d), wq, wkv, wqi, wki, wwi, *tabs)


def _compress_body(*refs, n_in, nblk, paged):
    if paged:
        refs = refs[1:]
    x_refs = refs[:n_in]
    pos_ref, w1_ref, w2_ref, o_ref = refs[n_in:n_in + 4]
    x_scr = refs[n_in + 4]
    rows = x_refs[0].shape[1]
    for p in range(n_in):
        for j in range(KV_COLS // LANES):
            x_scr[j, p * rows:(p + 1) * rows, :] = x_refs[p][0, :, j * LANES:(j + 1) * LANES]
    load = lambda l, col: x_scr[col // LANES, pl.ds(l, nblk, stride=BLOCK), :]
    outs = []
    for c in range(2):
        acc = jnp.zeros((N_KV * nblk, w1_ref.shape[-1]), F32)
        for l in range(BLOCK):
            pieces = []
            for kp in range(N_KV // 2):
                col = c * KV_DIM + kp * LANES
                a = (load(l, col) + pos_ref[l:l + 1, col:col + LANES]).astype(BF16)
                pieces += [a[:, :HEAD_DIM], a[:, HEAD_DIM:]]
            acc = acc + _dot(jnp.concatenate(pieces, axis=0), w1_ref[c, l])
        out = _dot(jax.nn.gelu(acc).astype(BF16), w2_ref[c])
        outs += [out[k * nblk:(k + 1) * nblk] for k in range(N_KV)]
    o_ref[...] = jnp.concatenate(outs, axis=1).astype(BF16)


def _compress_rows(x3, pos, w1, w2):
    g, r, _ = x3.shape
    nblk = r // BLOCK
    full = lambda a: pl.BlockSpec(a.shape, lambda i: (0,) * a.ndim)
    return pl.pallas_call(
        functools.partial(_compress_body, n_in=1, nblk=nblk, paged=False),
        grid=(g,),
        in_specs=[pl.BlockSpec((1, r, KV_COLS), lambda i: (i, 0, 0)), full(pos), full(w1), full(w2)],
        out_specs=pl.BlockSpec((nblk, KV_COLS), lambda i: (i, 0)),
        out_shape=jax.ShapeDtypeStruct((g * nblk, KV_COLS), BF16),
        scratch_shapes=[pltpu.VMEM((KV_COLS // LANES, r, LANES), F32)],
        compiler_params=_params(("parallel",)),
        name="compress_rows",
    )(x3, pos, w1, w2)


def _compress_paged_body(pt_ref, *refs, n_in):
    del pt_ref
    x_refs = refs[:n_in]
    pos_ref, w1_ref, w2_ref, o_ref, x_scr = refs[n_in:]
    per_page = N_KV * HEAD_DIM
    for p in range(n_in):
        for c in range(2):
            x_scr[c, p * per_page:(p + 1) * per_page, :] = x_refs[p][0, c].reshape(per_page, PAGE_SIZE)
    m = n_in * N_KV
    for c in range(2):
        acc = jnp.zeros((2 * m, w1_ref.shape[-1]), F32)
        for d in range(HEAD_DIM):
            a = (x_scr[c, pl.ds(d, m, stride=HEAD_DIM), :] + pos_ref[c, d:d + 1, :]).astype(BF16)
            acc = acc + _dot(jnp.concatenate([a[:, :BLOCK], a[:, BLOCK:]], axis=0), w1_ref[c, d])
        o_ref[0, c] = _dot(jax.nn.gelu(acc).astype(BF16), w2_ref[c]).astype(BF16)


def _compress_paged(pool, page_table, pos, w1, w2):
    db, n_pages = page_table.shape
    per_step = 2
    n_in = per_step * n_pages
    steps = db // per_step
    rows = 2 * n_in * N_KV
    full = lambda a: pl.BlockSpec(a.shape, lambda i, pt: (0,) * a.ndim)
    page = lambda p: pl.BlockSpec(
        (1, 2, N_KV, HEAD_DIM, PAGE_SIZE),
        lambda i, pt: (pt[per_step * i + p // n_pages, p % n_pages], 0, 0, 0, 0))
    out = pl.pallas_call(
        functools.partial(_compress_paged_body, n_in=n_in),
        grid_spec=pltpu.PrefetchScalarGridSpec(
            num_scalar_prefetch=1,
            grid=(steps,),
            in_specs=[page(p) for p in range(n_in)] + [full(pos), full(w1), full(w2)],
            out_specs=pl.BlockSpec((1, 2, rows, HEAD_DIM), lambda i, pt: (i, 0, 0, 0)),
            scratch_shapes=[pltpu.VMEM((2, n_in * N_KV * HEAD_DIM, PAGE_SIZE), F32)]),
        out_shape=jax.ShapeDtypeStruct((steps, 2, rows, HEAD_DIM), BF16),
        compiler_params=_params(("parallel",)),
        name="compress_paged",
    )(page_table, *([pool] * n_in), pos, w1, w2)
    out = out.reshape(steps, 2, 2, per_step, n_pages, N_KV, HEAD_DIM)
    return out.transpose(0, 3, 4, 2, 1, 5, 6).reshape(db * n_pages * 2, KV_COLS)


def _pair_rows(blk, half, tq):
    lane = _iota((1, LANES), 1)
    keep = (lane < HEAD_DIM) if half == 0 else (lane >= HEAD_DIM)
    zero = jnp.zeros((tq, LANES), blk.dtype)
    return jnp.concatenate(
        [jnp.where(keep, blk[:, g * LANES:(g + 1) * LANES], zero) for g in range(GROUP)], axis=0)


def _own_block():
    return jnp.right_shift(_iota((8, KV_DIM), 1), 6) == _iota((8, KV_DIM), 0)


def _head_rows(q_row):
    keep = _own_block()
    q32 = q_row.astype(F32)
    slabs = []
    for g in range(GROUP):
        qg = jnp.concatenate(
            [q32[:, (kp * GROUP + g) * LANES:(kp * GROUP + g + 1) * LANES] for kp in range(2)], axis=1)
        slabs.append(jnp.where(keep, jnp.broadcast_to(qg, (8, KV_DIM)), 0.0))
    return jnp.concatenate(slabs, axis=0).astype(q_row.dtype)


def _head_out(o_rows):
    keep = _own_block()
    per_g = [jnp.sum(jnp.where(keep, o_rows[g * 8:(g + 1) * 8], 0.0), axis=0, keepdims=True)
             for g in range(GROUP)]
    return jnp.concatenate(
        [per_g[g][:, kp * LANES:(kp + 1) * LANES] for kp in range(2) for g in range(GROUP)], axis=1)


def _select_top(score, n):
    idx = _iota(score.shape, 0).astype(F32)
    sel = jnp.zeros(score.shape, F32)
    for _ in range(n):
        m = jnp.max(score, axis=0, keepdims=True)
        j = jnp.min(jnp.where(score == m, idx, 1e9), axis=0, keepdims=True)
        hit = idx == j
        sel = jnp.where(hit, jnp.where(m >= 0.0, 1.0, 0.0), sel)
        score = jnp.where(hit, -2.0, score)
    return sel


def _softmax_masked_cols(s, mask):
    s = jnp.where(mask, s, NEG)
    p = jnp.where(mask, jnp.exp2(s - jnp.max(s, axis=0, keepdims=True)), 0.0)
    return p / jnp.maximum(jnp.sum(p, axis=0, keepdims=True), 1e-30)


def _block_scores(p_t, qpos_l, n_q):
    nb = p_t.shape[0]
    imp = p_t[:, 0:n_q]
    for g in range(1, GROUP):
        imp = imp + p_t[:, g * n_q:(g + 1) * n_q]
    blk = _iota((nb, 1), 0)
    cur = jnp.right_shift(qpos_l, 6)
    forced = (blk == 0) | (blk == cur) | (blk == cur - 1)
    score = jnp.where(forced, FORCE, imp)
    return jnp.where(blk <= cur, score, -1.0)


def _dot_tn(a, b):
    return lax.dot_general(a, b, (((0,), (0,)), ((), ())), preferred_element_type=F32)


def _attend_cols(streams, n_chunks):
    def body(c, carry):
        out = []
        for (q_rows, get_k, get_v, get_bias), (m, l, acc) in zip(streams, carry):
            st = _dot_nt(get_k(c), q_rows) + jnp.concatenate([get_bias(c)] * GROUP, axis=1)
            m_new = jnp.maximum(m, jnp.max(st, axis=0, keepdims=True))
            alpha = jnp.exp2(m - m_new)
            p = jnp.exp2(st - m_new)
            l = alpha * l + jnp.sum(p, axis=0, keepdims=True)
            acc = alpha * acc + _dot_tn(get_v(c), p.astype(BF16))
            out.append((m_new, l, acc))
        return tuple(out)

    init = tuple((jnp.full((1, s[0].shape[0]), NEG, F32), jnp.zeros((1, s[0].shape[0]), F32),
                  jnp.zeros((LANES, s[0].shape[0]), F32)) for s in streams)
    final = lax.fori_loop(0, n_chunks, body, init)
    return [acc / jnp.maximum(l, 1e-30) for _, l, acc in final]


def _nsa_core(q_blk, qr_blk, gates, kc, vc, get_k, get_v, n_chunks, ck, wk, wv, wstart, qpos_l):
    tq = q_blk.shape[0]
    nb = kc.shape[0]
    wlen = wk.shape[0]
    sub = _iota((LANES, 1), 0)
    qpos_lr = jnp.concatenate([qpos_l] * GROUP, axis=1)
    blk = _iota((nb, 1), 0)
    gates_t = gates.T
    qr_rows, o_cmps, streams = [], [], []
    for half in range(2):
        q_rows = _pair_rows(q_blk, half, tq)
        qr_rows.append(_pair_rows(qr_blk, half, tq))
        cmask = ((blk + 1) * BLOCK - 1) <= qpos_lr
        p_t = _softmax_masked_cols(_dot_nt(kc, q_rows), cmask)
        o_cmps.append(_dot_tn(vc, p_t.astype(BF16)))
        sel_t = _select_top(_block_scores(p_t, qpos_l, tq), min(N_SEL, nb)).astype(BF16)

        def slc_bias(c, sel_t=sel_t):
            kpos = c * ck + _iota((ck, 1), 0)
            expand = jnp.where(jnp.right_shift(kpos, 6) == _iota((1, nb), 1), 1.0, 0.0).astype(BF16)
            chosen = _dot(expand, sel_t)
            return jnp.where((chosen > 0.5) & (kpos <= qpos_l), 0.0, NEG)

        streams.append((qr_rows[half], get_k, get_v, slc_bias))
    o_slcs = _attend_cols(streams, n_chunks)
    per_half = []
    for half in range(2):
        o_cmp, o_slc = o_cmps[half], o_slcs[half]
        kwpos = wstart + _iota((wlen, 1), 0)
        dpos = qpos_lr - kwpos
        wmask = (dpos >= 0) & (dpos <= WINDOW) & (kwpos >= 0)
        o_win = _dot_tn(wv, _softmax_masked_cols(_dot_nt(wk, qr_rows[half]), wmask).astype(BF16))
        outs = []
        for g in range(GROUP):
            c0 = half * GROUP * 3 + g * 3
            sl = slice(g * tq, (g + 1) * tq)
            outs.append(gates_t[c0:c0 + 1] * o_cmp[:, sl] + gates_t[c0 + 1:c0 + 2] * o_slc[:, sl]
                        + gates_t[c0 + 2:c0 + 3] * o_win[:, sl])
        per_half.append(outs)
    return jnp.concatenate(
        [jnp.where(sub < HEAD_DIM, per_half[0][g], per_half[1][g]).T for g in range(GROUP)], axis=1)


def _nsa_prompt_body(q_ref, qr_ref, gt_ref, kc_ref, vc_ref, sk_ref, sv_ref, wk_ref, wv_ref, o_ref,
                     *, tq, ck):
    s0 = pl.program_id(2) * tq
    n_chunks = (s0 + tq + ck - 1) // ck
    wstart = pl.multiple_of(jnp.maximum(s0 - WINDOW, 0), tq)
    chunk = lambda ref: (lambda c: ref[pl.ds(pl.multiple_of(c * ck, ck), ck), :])
    out = _nsa_core(q_ref[...], qr_ref[...], gt_ref[...], kc_ref[...], vc_ref[...],
                    chunk(sk_ref), chunk(sv_ref), n_chunks, ck,
                    wk_ref[pl.ds(wstart, WINDOW + tq), :], wv_ref[pl.ds(wstart, WINDOW + tq), :],
                    wstart, s0 + _iota((1, tq), 1))
    o_ref[...] = out.astype(BF16)


def _nsa_attn_prompt(q, qr, gates, kvc, slc_b, win_b, b, t):
    tq = 128
    ck = min(512, t)
    nt = t // tq
    nb = t // BLOCK
    qspec = pl.BlockSpec((tq, 4 * LANES), lambda bi, kp, i: (bi * nt + i, kp))
    seq = lambda off: pl.BlockSpec((t, LANES), lambda bi, kp, i: (bi, kp + off))
    return pl.pallas_call(
        functools.partial(_nsa_prompt_body, tq=tq, ck=ck),
        grid=(b, 2, nt),
        in_specs=[qspec, qspec,
                  pl.BlockSpec((tq, LANES), lambda bi, kp, i: (bi * nt + i, kp)),
                  pl.BlockSpec((nb, LANES), lambda bi, kp, i: (bi, kp)),
                  pl.BlockSpec((nb, LANES), lambda bi, kp, i: (bi, kp + 2)),
                  seq(0), seq(2), seq(0), seq(2)],
        out_specs=qspec,
        out_shape=jax.ShapeDtypeStruct((b * t, Q_DIM), BF16),
        compiler_params=_params(("parallel", "parallel", "arbitrary")),
        name="nsa_attn_prompt",
    )(q, qr, gates, kvc, kvc, slc_b, slc_b, win_b, win_b)


def _tail_rows(row, n, dtype):
    w = row.shape[-1]
    first = _iota((n, w), 0) == 0
    return jnp.where(first, jnp.broadcast_to(row, (n, w)), 0.0).astype(dtype)


def _tail_cols(row, n):
    return _tail_rows(row, n, F32).T.reshape(2, N_KV, HEAD_DIM, n).astype(BF16)


def _feature_major(a):
    return jnp.moveaxis(a, -4, -1)


def _nsa_sample_body(pt_ref, q_ref, qr_ref, gt_ref, kcp_ref, kcn_ref, *rest,
                     n_pages, n_past, wb, tq, tail):
    del pt_ref
    pages = rest[:n_pages]
    slcn_ref, wst_ref, winn_ref, o_ref, kc_scr, slc_scr, win_scr = rest[n_pages:]
    nbp = kcp_ref.shape[0]
    kc_scr[...] = jnp.zeros_like(kc_scr)
    kc_scr[0:nbp, :] = kcp_ref[...]
    kc_scr[nbp:nbp + 16, :] = _tail_rows(kcn_ref[0].astype(F32), 16, BF16)
    for p in range(n_pages):
        slc_scr[:, :, :, p * PAGE_SIZE:(p + 1) * PAGE_SIZE] = pages[p][0].astype(BF16)
    slc_scr[:, :, :, n_past:n_past + tail] = _tail_cols(slcn_ref[0], tail)
    win_scr[:, :, :, 0:wb] = wst_ref[0].astype(BF16)
    win_scr[:, :, :, wb:wb + tail] = _tail_cols(winn_ref[0], tail)
    nb = kc_scr.shape[0]
    n_keys = n_past + tail
    q32, qr32 = _head_rows(q_ref[0]), _head_rows(qr_ref[0])
    blk = _iota((nb, 1), 0)
    gap = jnp.zeros((LANES - 8, KV_DIM), BF16)
    q_cols = jnp.concatenate(
        [x for g in range(GROUP) for x in (q32[g * 8:(g + 1) * 8], gap)], axis=0)
    cmask = ((blk + 1) * BLOCK - 1) <= n_past
    p_t = _softmax_masked_cols(_dot_nt(kc_scr[:, 0:KV_DIM], q_cols), cmask)
    p_rows = p_t.T
    p32 = jnp.concatenate([p_rows[g * LANES:g * LANES + 8] for g in range(GROUP)], axis=0)
    o_cmp = _dot(p32.astype(BF16), kc_scr[:, KV_DIM:KV_COLS])
    qpos_l = jnp.full((1, LANES), n_past, I32)
    sel8 = _select_top(_block_scores(p_t, qpos_l, LANES), min(N_SEL, nb)).T[0:8]
    sel32 = _tile_rows(sel8, GROUP).astype(BF16)
    kpos = _iota((1, n_keys), 1)
    expand = jnp.where(jnp.right_shift(kpos, 6) == blk, 1.0, 0.0).astype(BF16)
    smask = (_dot(sel32, expand) > 0.5) & (kpos <= n_past)
    p = _softmax_masked(_dot(qr32, slc_scr[0].reshape(KV_DIM, n_keys)), smask)
    o_slc = _dot_nt(p.astype(BF16), slc_scr[1].reshape(KV_DIM, n_keys))
    wlen = wb + tail
    kwpos = (n_past - wb) + _iota((1, wlen), 1)
    dpos = n_past - kwpos
    wmask = (dpos >= 0) & (dpos <= WINDOW) & (kwpos >= 0)
    p = _softmax_masked(_dot(qr32, win_scr[0].reshape(KV_DIM, wlen)), wmask)
    o_win = _dot_nt(p.astype(BF16), win_scr[1].reshape(KV_DIM, wlen))
    rows = q32.shape[0]
    r = _iota((rows, 2 * LANES), 0)
    k, g = r & 7, jnp.right_shift(r, 3)
    base = jnp.right_shift(k, 1) * LANES + (k & 1) * (GROUP * 3) + g * 3
    col = _iota((rows, 2 * LANES), 1)
    gb = jnp.broadcast_to(gt_ref[0], (rows, 2 * LANES))
    gate = lambda br: jnp.sum(jnp.where(col == base + br, gb, 0.0), axis=-1, keepdims=True)
    o_ref[0] = _head_out(gate(0) * o_cmp + gate(1) * o_slc + gate(2) * o_win).astype(BF16)


def _nsa_attn_sample(q, qr, gates, kvc_past, kvc_new, cache_slc, page_table, slc_new, win_state,
                     win_new):
    db, n_pages = page_table.shape
    n_past = n_pages * PAGE_SIZE
    wb = win_state.shape[-1]
    nbp = n_past // BLOCK
    tq, tail = 8, PAGE_SIZE
    nb = 64
    assert nb * BLOCK >= n_past + tail and nbp + 16 <= nb
    one = lambda w: pl.BlockSpec((1, 1, w), lambda i, pt: (i, 0, 0))
    fm = lambda n: (1, 2, N_KV, HEAD_DIM, n)
    page = lambda p: pl.BlockSpec(fm(PAGE_SIZE), lambda i, pt: (pt[i, p], 0, 0, 0, 0))
    r3 = lambda a: a.reshape(db, 1, a.shape[-1])
    return pl.pallas_call(
        functools.partial(_nsa_sample_body, n_pages=n_pages, n_past=n_past, wb=wb, tq=tq, tail=tail),
        grid_spec=pltpu.PrefetchScalarGridSpec(
            num_scalar_prefetch=1,
            grid=(db,),
            in_specs=[one(Q_DIM), one(Q_DIM), one(2 * LANES),
                      pl.BlockSpec((nbp, KV_COLS), lambda i, pt: (i, 0)), one(KV_COLS)]
                     + [page(p) for p in range(n_pages)]
                     + [one(KV_COLS), pl.BlockSpec(fm(wb), lambda i, pt: (i, 0, 0, 0, 0)),
                        one(KV_COLS)],
            out_specs=one(Q_DIM),
            scratch_shapes=[pltpu.VMEM((nb, KV_COLS), BF16),
                            pltpu.VMEM(fm(n_past + tail)[1:], BF16),
                            pltpu.VMEM(fm(wb + tail)[1:], BF16)]),
        out_shape=jax.ShapeDtypeStruct((db, 1, Q_DIM), BF16),
        compiler_params=_params(("parallel",)),
        name="nsa_attn_sample",
    )(page_table, r3(q), r3(qr), r3(gates), kvc_past, r3(kvc_new), *([cache_slc] * n_pages),
      r3(slc_new), win_state, r3(win_new)).reshape(db, Q_DIM)


def _dsa_select(qi_blk, wi, get_ki, n_chunks, ck, qpos_t, key_scr, topk, kt, same_rows=False):
    tq = qi_blk.shape[0]
    qi_rows = jnp.concatenate(
        [qi_blk[:, h * LANES:h * LANES + IDX_DIM] for h in range(N_IDX)], axis=0)
    wcol = [wi[:, h:h + 1] for h in range(N_IDX)]

    def loop(fn, init):
        if isinstance(n_chunks, int):
            for c in range(n_chunks):
                init = fn(c, init)
            return init
        return lax.fori_loop(0, n_chunks, fn, init)

    def kpos_of(c):
        return c * ck + _iota((1, ck), 1)

    def score_chunk(c, carry):
        sc = _dot(qi_rows, get_ki(c)) if kt else _dot_nt(qi_rows, get_ki(c))
        score = jnp.maximum(sc[0:tq], 0.0) * wcol[0]
        for h in range(1, N_IDX):
            score = score + jnp.maximum(sc[h * tq:(h + 1) * tq], 0.0) * wcol[h]
        score = jnp.where(kpos_of(c) <= qpos_t, score, NEG)
        score = jnp.where(score == 0.0, 0.0, score)
        bits = lax.bitcast_convert_type(score, I32)
        key_scr[c] = jnp.where(bits < 0, bits ^ 0x7FFFFFFF, bits)
        return carry

    loop(score_chunk, 0)

    def count(pred):
        def fn(c, acc):
            hit = jnp.where(pred(key_scr[c], kpos_of(c)), 1.0, 0.0)
            part = hit[:, 0:LANES]
            for j in range(1, ck // LANES):
                part = part + hit[:, j * LANES:(j + 1) * LANES]
            return acc + part
        return jnp.sum(loop(fn, jnp.zeros((tq, LANES), F32)), axis=-1, keepdims=True)

    kf = float(topk)
    if same_rows:
        assert tq == 8
        row = _iota((tq, 1), 0)
        digit = jnp.minimum(row + 1, 7)

        def digit_step(it, lo):
            shift = jnp.maximum(29 - 3 * it, 0)
            cand = lo + jnp.left_shift(digit, shift)
            ok = (count(lambda k, p: k >= cand) >= kf) & (row < 7)
            kept = jnp.sum(jnp.where(ok, 1.0, 0.0), axis=0, keepdims=True).astype(I32)
            return lo + jnp.left_shift(kept, shift)

        thr = lax.fori_loop(0, 11, digit_step, jnp.full((tq, 1), INT_MIN, I32))
    else:
        lo = jnp.where(count(lambda k, p: k >= 0) >= kf, 0, INT_MIN).astype(I32)

        def bit_step(it, lo):
            cand = lo + jnp.left_shift(jnp.int32(1), 30 - it)
            return jnp.where(count(lambda k, p: k >= cand) >= kf, cand, lo)

        thr = lax.fori_loop(0, 31, bit_step, lo)
    need = kf - count(lambda k, p: k > thr)
    surplus = jnp.max(count(lambda k, p: k >= thr)) > kf

    def idx_step(it, lo):
        mid = lo + jnp.left_shift(jnp.int32(1), 13 - it)
        enough = count(lambda k, p: (k == thr) & (p <= mid)) >= need
        return jnp.where(enough, lo, mid)

    jmax = lax.cond(
        surplus,
        lambda: lax.fori_loop(0, 14, idx_step, jnp.full((tq, 1), -1, I32)) + 1,
        lambda: jnp.full((tq, 1), 2 ** 14, I32))

    def bias_chunk(c, carry):
        k = key_scr[c]
        p = kpos_of(c)
        keep = ((k > thr) | ((k == thr) & (p <= jmax))) & (p <= qpos_t)
        key_scr[c] = lax.bitcast_convert_type(jnp.where(keep, 0.0, NEG), I32)
        return carry

    loop(bias_chunk, 0)


def _dsa_select_cols(qi_blk, wi, get_ki, n_chunks, ck, qpos_l, key_scr, topk):
    tq = qi_blk.shape[0]
    qi_rows = jnp.concatenate(
        [qi_blk[:, h * LANES:h * LANES + IDX_DIM] for h in range(N_IDX)], axis=0)
    wi_t = wi.T

    def loop(fn, init):
        return lax.fori_loop(0, n_chunks, fn, init)

    def kpos_of(c):
        return c * ck + _iota((ck, 1), 0)

    def score_chunk(c, carry):
        sc = _dot_nt(get_ki(c), qi_rows)
        score = jnp.maximum(sc[:, 0:tq], 0.0) * wi_t[0:1]
        for h in range(1, N_IDX):
            score = score + jnp.maximum(sc[:, h * tq:(h + 1) * tq], 0.0) * wi_t[h:h + 1]
        score = jnp.where(kpos_of(c) <= qpos_l, score, NEG)
        score = jnp.where(score == 0.0, 0.0, score)
        bits = lax.bitcast_convert_type(score, I32)
        key_scr[c] = jnp.where(bits < 0, bits ^ 0x7FFFFFFF, bits)
        return carry

    loop(score_chunk, 0)

    def count(pred):
        def fn(c, acc):
            hit = jnp.where(pred(key_scr[c], kpos_of(c)), 1.0, 0.0)
            part = jnp.sum(hit.reshape(8, ck // 64, 8, tq), axis=1)
            return acc + jnp.sum(part, axis=0)
        return jnp.sum(loop(fn, jnp.zeros((8, tq), F32)), axis=0, keepdims=True)

    kf = float(topk)
    lo = jnp.where(count(lambda k, p: k >= 0) >= kf, 0, INT_MIN).astype(I32)

    def bit_step(it, lo):
        cand = lo + jnp.left_shift(jnp.int32(1), 30 - it)
        return jnp.where(count(lambda k, p: k >= cand) >= kf, cand, lo)

    thr = lax.fori_loop(0, 31, bit_step, lo)
    need = kf - count(lambda k, p: k > thr)
    surplus = jnp.max(count(lambda k, p: k >= thr)) > kf

    def idx_step(it, lo):
        mid = lo + jnp.left_shift(jnp.int32(1), 13 - it)
        enough = count(lambda k, p: (k == thr) & (p <= mid)) >= need
        return jnp.where(enough, lo, mid)

    jmax = lax.cond(
        surplus,
        lambda: lax.fori_loop(0, 14, idx_step, jnp.full((1, tq), -1, I32)) + 1,
        lambda: jnp.full((1, tq), 2 ** 14, I32))

    def bias_chunk(c, carry):
        k = key_scr[c]
        p = kpos_of(c)
        keep = ((k > thr) | ((k == thr) & (p <= jmax))) & (p <= qpos_l)
        key_scr[c] = lax.bitcast_convert_type(jnp.where(keep, 0.0, NEG), I32)
        return carry

    loop(bias_chunk, 0)


def _dsa_prompt_body(q_ref, qi_ref, wi_ref, ki_ref, k0_ref, k1_ref, v0_ref, v1_ref, o_ref, key_scr,
                     *, tq, ck, topk):
    s0 = pl.program_id(1) * tq
    qpos_t = s0 + _iota((tq, 1), 0)
    n_chunks = (s0 + tq + ck - 1) // ck
    rows = lambda c: pl.ds(pl.multiple_of(c * ck, ck), ck)
    k_refs, v_refs = (k0_ref, k1_ref), (v0_ref, v1_ref)
    del qpos_t
    _dsa_select_cols(qi_ref[...], wi_ref[...], lambda c: ki_ref[rows(c), :], n_chunks, ck,
                     s0 + _iota((1, tq), 1), key_scr, topk)
    get_bias = lambda c: lax.bitcast_convert_type(key_scr[c], F32)
    q_blk = q_ref[...]
    sub = _iota((LANES, 1), 0)
    streams = [(_pair_rows(q_blk[:, kp * 4 * LANES:(kp + 1) * 4 * LANES], half, tq),
                lambda c, kp=kp: k_refs[kp][rows(c), :], lambda c, kp=kp: v_refs[kp][rows(c), :],
                get_bias) for kp in range(2) for half in range(2)]
    outs = _attend_cols(streams, n_chunks)
    for kp in range(2):
        for g in range(GROUP):
            sl = slice(g * tq, (g + 1) * tq)
            cols = slice((kp * GROUP + g) * LANES, (kp * GROUP + g + 1) * LANES)
            o_ref[:, cols] = jnp.where(sub < HEAD_DIM, outs[2 * kp][:, sl],
                                       outs[2 * kp + 1][:, sl]).T.astype(BF16)


def _dsa_attn_prompt(q, qi, wi, ki_b, kv_b, b, t):
    tq = 128
    ck = min(512, t)
    nt = t // tq
    row = lambda w: pl.BlockSpec((tq, w), lambda bi, i: (bi * nt + i, 0))
    seq = lambda j: pl.BlockSpec((t, LANES), lambda bi, i: (bi, j))
    return pl.pallas_call(
        functools.partial(_dsa_prompt_body, tq=tq, ck=ck, topk=min(DSA_TOPK, t // 4)),
        grid=(b, nt),
        in_specs=[row(Q_DIM), row(N_IDX * LANES), row(LANES),
                  pl.BlockSpec((t, IDX_DIM), lambda bi, i: (bi, 0)),
                  seq(0), seq(1), seq(2), seq(3)],
        out_specs=row(Q_DIM),
        out_shape=jax.ShapeDtypeStruct((b * t, Q_DIM), BF16),
        scratch_shapes=[pltpu.VMEM((t // ck, ck, tq), I32)],
        compiler_params=_params(("parallel", "arbitrary")),
        name="dsa_attn_prompt",
    )(q, qi, wi, ki_b, kv_b, kv_b, kv_b, kv_b)


def _dsa_sample_body(pt_ref, q_ref, qi_ref, wi_ref, *rest, n_pages, n_past, tq, tail, topk):
    del pt_ref
    kv_pages = rest[:n_pages]
    ki_pages = rest[n_pages:2 * n_pages]
    kvn_ref, kin_ref, o_ref, kv_scr, ki_scr, key_scr = rest[2 * n_pages:]
    for p in range(n_pages):
        cols = slice(p * PAGE_SIZE, (p + 1) * PAGE_SIZE)
        kv_scr[:, :, :, cols] = kv_pages[p][0].astype(BF16)
        ki_scr[:, cols] = ki_pages[p][0].astype(BF16)
    kv_scr[:, :, :, n_past:n_past + tail] = _tail_cols(kvn_ref[0], tail)
    ki_scr[:, n_past:n_past + tail] = _tail_rows(kin_ref[0], tail, F32).T[0:IDX_DIM].astype(BF16)
    qpos_t = jnp.full((tq, 1), n_past, I32)
    ck = n_past + tail
    _dsa_select(jnp.broadcast_to(qi_ref[0], (tq, N_IDX * LANES)), jnp.broadcast_to(wi_ref[0], (tq, LANES)),
                lambda c: ki_scr[...], 1, ck, qpos_t, key_scr, topk, True, same_rows=True)
    mask = lax.bitcast_convert_type(key_scr[0, 0:1, :], F32) == 0.0
    p = _softmax_masked(_dot(_head_rows(q_ref[0]), kv_scr[0].reshape(KV_DIM, ck)), mask)
    o_ref[0] = _head_out(_dot_nt(p.astype(BF16), kv_scr[1].reshape(KV_DIM, ck))).astype(BF16)


def _dsa_attn_sample(q, qi, wi, cache_kv, cache_idx, page_table, kv_new, ki_new):
    db, n_pages = page_table.shape
    n_past = n_pages * PAGE_SIZE
    tq, tail = 8, PAGE_SIZE
    one = lambda w: pl.BlockSpec((1, 1, w), lambda i, pt: (i, 0, 0))
    kv_page = lambda p: pl.BlockSpec((1, 2, N_KV, HEAD_DIM, PAGE_SIZE),
                                     lambda i, pt: (pt[i, p], 0, 0, 0, 0))
    ki_page = lambda p: pl.BlockSpec((1, IDX_DIM, PAGE_SIZE), lambda i, pt: (pt[i, p], 0, 0))
    r3 = lambda a: a.reshape(db, 1, a.shape[-1])
    return pl.pallas_call(
        functools.partial(_dsa_sample_body, n_pages=n_pages, n_past=n_past, tq=tq, tail=tail,
                          topk=min(DSA_TOPK, (n_past + 1) // 4)),
        grid_spec=pltpu.PrefetchScalarGridSpec(
            num_scalar_prefetch=1,
            grid=(db,),
            in_specs=[one(Q_DIM), one(N_IDX * LANES), one(LANES)]
                     + [kv_page(p) for p in range(n_pages)]
                     + [ki_page(p) for p in range(n_pages)]
                     + [one(KV_COLS), one(LANES)],
            out_specs=one(Q_DIM),
            scratch_shapes=[pltpu.VMEM((2, N_KV, HEAD_DIM, n_past + tail), BF16),
                            pltpu.VMEM((IDX_DIM, n_past + tail), BF16),
                            pltpu.VMEM((1, tq, n_past + tail), I32)]),
        out_shape=jax.ShapeDtypeStruct((db, 1, Q_DIM), BF16),
        compiler_params=_params(("parallel",)),
        name="dsa_attn_sample",
    )(page_table, r3(q), r3(qi), r3(wi), *([cache_kv] * n_pages), *([cache_idx] * n_pages),
      r3(kv_new), r3(ki_new)).reshape(db, Q_DIM)


def _rope_tables(pos):
    half = ROT_DIM // 2
    n = pos.shape[0]
    inv = ROPE_THETA ** (-jnp.arange(half, dtype=F32) * 2.0 / ROT_DIM)
    ang = pos.astype(F32)[:, None] * inv[None, :]
    cos, sin = jnp.cos(ang), jnp.sin(ang)
    z8 = jnp.zeros((n, half), F32)
    rest = HEAD_DIM - ROT_DIM
    c = jnp.concatenate([cos, cos, jnp.ones((n, rest), F32)], axis=1)
    s1 = jnp.concatenate([-sin, z8, jnp.zeros((n, rest), F32)], axis=1)
    s2 = jnp.concatenate([z8, sin, jnp.zeros((n, rest), F32)], axis=1)
    rep = LANES // HEAD_DIM
    return tuple(jnp.tile(a, (1, rep)) for a in (c, s1, s2))


def _pair_perm():
    cols = []
    for kp in range(N_KV // 2):
        for g in range(GROUP):
            for half in range(2):
                h = (2 * kp + half) * GROUP + g
                cols.extend(range(h * HEAD_DIM, (h + 1) * HEAD_DIM))
    return jnp.array(cols, dtype=I32)


def _rows_major(a):
    n, _, rows = a.shape
    return jnp.moveaxis(a.reshape(n, 2, N_KV, HEAD_DIM, rows), -1, 1)


def _pad_cols(w, n):
    return jnp.pad(w, ((0, 0), (0, n - w.shape[1])))


def _nsa_layer(xp, xs, g, w_in, w_out, cmp_pos, cmp_w1, cmp_w2, cache_cmp, cache_slc, win_state,
               page_table, tabs_p, tabs_s, b, t):
    db = xs.shape[0]
    perm = _pair_perm()
    wq = w_in[:, :Q_DIM][:, perm].astype(BF16)
    wkv = w_in[:, Q_DIM:Q_DIM + 3 * KV_COLS].astype(BF16)
    wg = w_in[:, Q_DIM + 3 * KV_COLS:]
    wgt = jnp.concatenate([_pad_cols(wg[:, kp * 2 * GROUP * 3:(kp + 1) * 2 * GROUP * 3], LANES)
                           for kp in range(N_KV // 2)], axis=1).astype(BF16)
    wo = w_out[perm, :].astype(BF16)
    pos = jnp.concatenate([jnp.tile(cmp_pos[:, c, :], (1, N_KV)) for c in range(2)], axis=1)
    w1 = cmp_w1.astype(BF16)
    w2 = cmp_w2.astype(BF16)

    q, qr, gates, kcmp, _, _, slc_b, win_b, cmp_t, slc_t, win_t = _nsa_proj(
        xp, g, wq, wkv, wgt, tabs_p, t // min(512, t), b)
    kvc = _compress_rows(kcmp.reshape(b, t, KV_COLS), pos, w1, w2)
    o = _nsa_attn_prompt(q, qr, gates, kvc, slc_b, win_b, b, t)
    xp = _oproj(xp, o, wo)
    qs, qrs, gs, scmp, sslc, swin, _, _, scmp_t, sslc_t, swin_t = _nsa_proj(
        xs, g, wq, wkv, wgt, tabs_s, 1, 1)
    pos_fm = jnp.tile(cmp_pos.transpose(1, 2, 0), (1, 1, PAGE_SIZE // BLOCK))
    w1_fm = cmp_w1.transpose(0, 2, 1, 3).astype(BF16)
    kvc_past = _compress_paged(_feature_major(cache_cmp), page_table, pos_fm, w1_fm, w2)
    new_rows = jnp.pad(scmp[:, None, :], ((0, 0), (0, BLOCK - 1), (0, 0)))
    kvc_new = _compress_rows(new_rows.reshape(2, db // 2 * BLOCK, KV_COLS), pos, w1, w2)
    os_ = _nsa_attn_sample(qs, qrs, gs, kvc_past, kvc_new, _feature_major(cache_slc), page_table,
                           sslc, _feature_major(win_state), swin)
    xs = _oproj(xs, os_, wo)
    wb = min(WINDOW, t)
    win_s = jnp.concatenate([win_state, swin.reshape(db, 1, 2, N_KV, HEAD_DIM)], axis=1)[:, 1:]
    outs = (_rows_major(cmp_t), _rows_major(scmp_t).reshape(db, 1, 2, N_KV, HEAD_DIM),
            _rows_major(slc_t), _rows_major(sslc_t).reshape(db, 1, 2, N_KV, HEAD_DIM),
            _rows_major(win_t[:, :, t - wb:]), win_s)
    return xp, xs, outs


def _dsa_layer(xp, xs, g, w_in, w_out, cache_kv, cache_idx, page_table, tabs_p, tabs_s, b, t):
    db = xs.shape[0]
    perm = _pair_perm()
    wq = w_in[:, :Q_DIM][:, perm].astype(BF16)
    o1 = Q_DIM + KV_COLS
    wkv = w_in[:, Q_DIM:o1].astype(BF16)
    o2 = o1 + N_IDX * IDX_DIM
    wqi = jnp.concatenate([_pad_cols(w_in[:, o1 + h * IDX_DIM:o1 + (h + 1) * IDX_DIM], LANES)
                           for h in range(N_IDX)], axis=1).astype(BF16)
    wki = _pad_cols(w_in[:, o2:o2 + IDX_DIM], LANES).astype(BF16)
    wwi = _pad_cols(w_in[:, o2 + IDX_DIM:], LANES).astype(BF16)
    wo = w_out[perm, :].astype(BF16)

    q, _, kv_b, qi, _, ki_b, wi, kv_t, ki_t = _dsa_proj(
        xp, g, wq, wkv, wqi, wki, wwi, tabs_p, t // min(512, t), b)
    o = _dsa_attn_prompt(q, qi, wi, ki_b, kv_b, b, t)
    xp = _oproj(xp, o, wo)
    qs, kvs, _, qis, kis, _, wis, kvs_t, kis_t = _dsa_proj(
        xs, g, wq, wkv, wqi, wki, wwi, tabs_s, 1, 1)
    os_ = _dsa_attn_sample(qs, qis, wis, _feature_major(cache_kv), jnp.swapaxes(cache_idx, -1, -2),
                           page_table, kvs, _pad_cols(kis, LANES))
    xs = _oproj(xs, os_, wo)
    outs = (_rows_major(kv_t), _rows_major(kvs_t).reshape(db, 1, 2, N_KV, HEAD_DIM),
            jnp.swapaxes(ki_t, -1, -2), jnp.swapaxes(kis_t, -1, -2).reshape(db, 1, IDX_DIM))
    return xp, xs, outs


def kernel(x_prompt, x_sample, cache_nsa_cmp, cache_nsa_slc, state_nsa_win, cache_dsa_kv, cache_dsa_idx,
           page_table, norm_g, w_ffn_in, w_ffn_out, w_nsa_in, w_nsa_out, nsa_cmp_pos, nsa_cmp_w1,
           nsa_cmp_w2, w_dsa_in, w_dsa_out, final_norm_g):
    b, t, d = x_prompt.shape
    db, ds, _ = x_sample.shape
    assert ds == 1
    depth = norm_g.shape[0]
    n_past = page_table.shape[1] * PAGE_SIZE
    xp = x_prompt.reshape(b * t, d)
    xs = x_sample.reshape(db, d)
    tabs_p = _rope_tables(jnp.arange(t))
    tabs_s = _rope_tables(jnp.full((db,), n_past, I32))
    w_in_b = w_ffn_in.astype(BF16)
    w_out_b = w_ffn_out.astype(BF16)
    nsa_outs, dsa_outs = [], []
    for layer in range(depth):
        kind, j = layer % 2, layer // 2
        xp = _ffn(xp, norm_g[layer, 0], w_in_b[layer, 0], w_out_b[layer, 0])
        xs = _ffn(xs, norm_g[layer, 0], w_in_b[layer, 0], w_out_b[layer, 0])
        if kind == 0:
            xp, xs, outs = _nsa_layer(xp, xs, norm_g[layer, 1], w_nsa_in[j], w_nsa_out[j],
                                      nsa_cmp_pos[j], nsa_cmp_w1[j], nsa_cmp_w2[j], cache_nsa_cmp[j],
                                      cache_nsa_slc[j], state_nsa_win[j], page_table, tabs_p, tabs_s,
                                      b, t)
            nsa_outs.append(outs)
        else:
            xp, xs, outs = _dsa_layer(xp, xs, norm_g[layer, 1], w_dsa_in[j], w_dsa_out[j],
                                      cache_dsa_kv[j], cache_dsa_idx[j], page_table, tabs_p, tabs_s,
                                      b, t)
            dsa_outs.append(outs)
        fg = final_norm_g if layer == depth - 1 else None
        xp = _ffn(xp, norm_g[layer, 2], w_in_b[layer, 1], w_out_b[layer, 1], fg)
        xs = _ffn(xs, norm_g[layer, 2], w_in_b[layer, 1], w_out_b[layer, 1], fg)
    stack = lambda outs, k: jnp.stack([o[k] for o in outs])
    return (xp.reshape(b, t, d), xs.reshape(db, ds, d),
            stack(nsa_outs, 0), stack(nsa_outs, 1), stack(nsa_outs, 2), stack(nsa_outs, 3),
            stack(nsa_outs, 4), stack(nsa_outs, 5),
            stack(dsa_outs, 0), stack(dsa_outs, 1), stack(dsa_outs, 2), stack(dsa_outs, 3))
```
